```python
import math
import jax, jax.numpy as jnp
from jax import lax
import numpy as np

D_MODEL = 2048
BATCH = 16
SEQ = 2048
DEPTH = 4
DEC_BATCH = 1
DEC_SEQ = 8192
PAST_LEN = 128

HEAD_DIM = 128
N_HEADS_A = 8
N_KV_A = 2
GROUP_A = N_HEADS_A // N_KV_A
WINDOW = 128
BLOCK = 128
N_BUCKETS = 32
MAX_DISTANCE = 128
N_HEADS_B = 8
Q_LORA = 512
KV_LORA = 256
D_NOPE = 128
D_ROPE = 64
D_V = 128
ROPE_THETA = 10000.0
MIX_WIDTH = N_HEADS_A * HEAD_DIM + N_HEADS_B * D_V
D_FF = 5632
CONV_WIDTH = 3
ALPHA = (2 * DEPTH) ** 0.25
BETA = (8 * DEPTH) ** -0.25
LN_EPS = 1e-5
RMS_EPS = 1e-6
NEG_INF = -1e30
SPLITS = (N_HEADS_A * HEAD_DIM, N_KV_A * HEAD_DIM, N_KV_A * HEAD_DIM, Q_LORA, KV_LORA, D_ROPE)
IN_COLS = N_HEADS_A * HEAD_DIM + 2 * N_KV_A * HEAD_DIM + Q_LORA + KV_LORA + D_ROPE

kernel_name = "hymba_swa_mla_deepnorm_encoder"


def layer_norm(x, g, b):
    xf = x.astype(jnp.float32)
    mu = xf.mean(-1, keepdims=True)
    var = jnp.square(xf - mu).mean(-1, keepdims=True)
    y = (xf - mu) * lax.rsqrt(var + LN_EPS) * g.astype(jnp.float32) + b.astype(jnp.float32)
    return y.astype(x.dtype)


def rms_norm(x, g):
    xf = x.astype(jnp.float32)
    y = xf * lax.rsqrt(jnp.mean(xf * xf, -1, keepdims=True) + RMS_EPS) * g.astype(jnp.float32)
    return y.astype(x.dtype)


def t5_bucket(rel):
    half = N_BUCKETS // 2
    max_exact = half // 2
    ret = (rel > 0).astype(jnp.int32) * half
    n = jnp.abs(rel)
    large = max_exact + (jnp.log(jnp.maximum(n, 1).astype(jnp.float32) / max_exact)
                         / math.log(MAX_DISTANCE / max_exact) * (half - max_exact)).astype(jnp.int32)
    large = jnp.minimum(large, half - 1)
    return ret + jnp.where(n < max_exact, n, large)


def rope_tables(S, dtype):
    inv = 1.0 / (ROPE_THETA ** (jnp.arange(0, D_ROPE, 2, dtype=jnp.float32) / D_ROPE))
    ang = jnp.arange(S, dtype=jnp.float32)[:, None] * inv[None, :]
    return jnp.cos(ang).astype(dtype), jnp.sin(ang).astype(dtype)


def apply_rope(x, cos, sin):
    x1, x2 = jnp.split(x, 2, axis=-1)
    return jnp.concatenate([x1 * cos - x2 * sin, x2 * cos + x1 * sin], axis=-1)


def window_gqa(q, k, v, sink, rel_bias):
    B, S = q.shape[0], q.shape[1]
    nb = S // BLOCK
    pad = ((0, 0), (BLOCK, BLOCK), (0, 0), (0, 0))
    kp = jnp.pad(k, pad).reshape(B, nb + 2, BLOCK, N_KV_A, HEAD_DIM)
    vp = jnp.pad(v, pad).reshape(B, nb + 2, BLOCK, N_KV_A, HEAD_DIM)
    kw = jnp.concatenate([kp[:, :-2], kp[:, 1:-1], kp[:, 2:]], axis=2)
    vw = jnp.concatenate([vp[:, :-2], vp[:, 1:-1], vp[:, 2:]], axis=2)
    qb = q.reshape(B, nb, BLOCK, N_KV_A, GROUP_A, HEAD_DIM)
    s = jnp.einsum('bnqhgd,bnkhd->bnhgqk', qb, kw).astype(jnp.float32) * (HEAD_DIM ** -0.5)
    qi = jnp.arange(BLOCK, dtype=jnp.int32)
    ki = jnp.arange(3 * BLOCK, dtype=jnp.int32)
    rel = ki[None, :] - BLOCK - qi[:, None]
    bias = rel_bias.astype(jnp.float32)[t5_bucket(rel)]
    bias = bias.transpose(2, 0, 1).reshape(N_KV_A, GROUP_A, BLOCK, 3 * BLOCK)
    kpos = jnp.arange(nb, dtype=jnp.int32)[:, None] * BLOCK - BLOCK + ki[None, :]
    valid = (jnp.abs(rel) <= WINDOW)[None] & ((kpos >= 0) & (kpos < S))[:, None, :]
    s = jnp.where(valid[None, :, None, None], s + bias[None, None], NEG_INF)
    sk = jnp.broadcast_to(sink.astype(jnp.float32).reshape(N_KV_A, GROUP_A, 1, 1), s.shape[:-1] + (1,))
    p = jax.nn.softmax(jnp.concatenate([s, sk], axis=-1), axis=-1)[..., :-1]
    o = jnp.einsum('bnhgqk,bnkhd->bnqhgd', p.astype(v.dtype), vw)
    return o.reshape(B, S, N_HEADS_A * HEAD_DIM)


def latent_attention(c_q, c_kv, k_rope, q_norm_g, w_uq, kv_norm_g, w_ukv):
    B, S = c_q.shape[0], c_q.shape[1]
    nb = S // BLOCK
    q = (rms_norm(c_q, q_norm_g) @ w_uq).reshape(B, S, N_HEADS_B, D_NOPE + D_ROPE)
    kv = (rms_norm(c_kv, kv_norm_g) @ w_ukv).reshape(B, S, N_HEADS_B, D_NOPE + D_V)
    q_nope, q_rope = q[..., :D_NOPE], q[..., D_NOPE:]
    k_nope, v = kv[..., :D_NOPE], kv[..., D_NOPE:]
    cos, sin = rope_tables(S, q.dtype)
    q_rope = apply_rope(q_rope, cos[:, None, :], sin[:, None, :])
    k_rope = apply_rope(k_rope, cos, sin)
    scale = (D_NOPE + D_ROPE) ** -0.5
    qn = q_nope.reshape(B, nb, BLOCK, N_HEADS_B, D_NOPE).transpose(1, 0, 2, 3, 4)
    qr = q_rope.reshape(B, nb, BLOCK, N_HEADS_B, D_ROPE).transpose(1, 0, 2, 3, 4)

    def query_block(args):
        qn_b, qr_b = args
        s = (jnp.einsum('bqhd,bkhd->bhqk', qn_b, k_nope)
             + jnp.einsum('bqhd,bkd->bhqk', qr_b, k_rope)).astype(jnp.float32) * scale
        p = jax.nn.softmax(s, axis=-1)
        return jnp.einsum('bhqk,bkhd->bqhd', p.astype(v.dtype), v)

    o = lax.map(query_block, (qn, qr))
    return o.transpose(1, 0, 2, 3, 4).reshape(B, S, N_HEADS_B * D_V)


def conv_glu(x, w_up, conv_w, conv_b, w_down):
    u = x @ w_up
    up = jnp.pad(u, ((0, 0), (1, 1), (0, 0)))
    u = up[:, :-2] * conv_w[0] + up[:, 1:-1] * conv_w[1] + up[:, 2:] * conv_w[2] + conv_b
    g, val = u[..., :D_FF], u[..., D_FF:]
    return (jax.nn.silu(g) * val) @ w_down


def encoder_layer(x, rel_bias, w_in, sink, q_norm_g, w_uq, kv_norm_g, w_ukv, w_o,
                  ln1_g, ln1_b, w_up, conv_w, conv_b, w_down, ln2_g, ln2_b):
    B, S = x.shape[0], x.shape[1]
    h = x @ w_in
    cuts = [int(c) for c in np.cumsum(SPLITS)[:-1]]
    qa, ka, va, c_q, c_kv, k_rope = jnp.split(h, cuts, axis=-1)
    oa = window_gqa(qa.reshape(B, S, N_HEADS_A, HEAD_DIM),
                    ka.reshape(B, S, N_KV_A, HEAD_DIM),
                    va.reshape(B, S, N_KV_A, HEAD_DIM), sink, rel_bias)
    ob = latent_attention(c_q, c_kv, k_rope, q_norm_g, w_uq, kv_norm_g, w_ukv)
    attn = jnp.concatenate([oa, ob], axis=-1) @ w_o
    x = layer_norm(ALPHA * x + attn, ln1_g, ln1_b)
    x = layer_norm(ALPHA * x + conv_glu(x, w_up, conv_w, conv_b, w_down), ln2_g, ln2_b)
    return x


def run_trunk(x, rel_bias, w_in, sink, q_norm_g, w_uq, kv_norm_g, w_ukv, w_o,
              ln1_g, ln1_b, w_up, conv_w, conv_b, w_down, ln2_g, ln2_b):
    for l in range(DEPTH):
        x = encoder_layer(x, rel_bias, w_in[l], sink[l], q_norm_g[l], w_uq[l], kv_norm_g[l],
                          w_ukv[l], w_o[l], ln1_g[l], ln1_b[l], w_up[l], conv_w[l], conv_b[l],
                          w_down[l], ln2_g[l], ln2_b[l])
    return x


def setup_inputs(seed: int = 0) -> dict:
    key = jax.random.key(seed)
    ks = jax.random.split(key, 20)
    f32 = jnp.float32
    nrm = lambda k, shape, scale: jax.random.normal(k, shape, f32) * scale
    col_scale = jnp.concatenate([
        jnp.ones((SPLITS[0] + SPLITS[1],), f32),
        jnp.full((SPLITS[2],), BETA, f32),
        jnp.ones((SPLITS[3] + SPLITS[4] + SPLITS[5],), f32)])
    w_in = nrm(ks[2], (DEPTH, D_MODEL, IN_COLS), D_MODEL ** -0.5) * col_scale
    ukv_scale = jnp.concatenate([jnp.ones((D_NOPE,), f32), jnp.full((D_V,), BETA, f32)])
    w_ukv = (nrm(ks[6], (DEPTH, KV_LORA, N_HEADS_B, D_NOPE + D_V), KV_LORA ** -0.5)
             * ukv_scale).reshape(DEPTH, KV_LORA, N_HEADS_B * (D_NOPE + D_V))
    return {
        "x_prompt": nrm(ks[0], (BATCH, SEQ, D_MODEL), 1.0),
        "x_sample": nrm(ks[1], (DEC_BATCH, DEC_SEQ, D_MODEL), 1.0),
        "rel_bias": nrm(ks[3], (N_BUCKETS, N_HEADS_A), 0.5),
        "w_in": w_in,
        "sink": nrm(ks[4], (DEPTH, N_HEADS_A), 0.5),
        "q_norm_g": 1.0 + nrm(ks[5], (DEPTH, Q_LORA), 0.02),
        "w_uq": nrm(ks[7], (DEPTH, Q_LORA, N_HEADS_B * (D_NOPE + D_ROPE)), Q_LORA ** -0.5),
        "kv_norm_g": 1.0 + nrm(ks[8], (DEPTH, KV_LORA), 0.02),
        "w_ukv": w_ukv,
        "w_o": nrm(ks[9], (DEPTH, MIX_WIDTH, D_MODEL), BETA * MIX_WIDTH ** -0.5),
        "ln1_g": 1.0 + nrm(ks[10], (DEPTH, D_MODEL), 0.02),
        "ln1_b": nrm(ks[11], (DEPTH, D_MODEL), 0.02),
        "w_up": nrm(ks[12], (DEPTH, D_MODEL, 2 * D_FF), BETA * D_MODEL ** -0.5),
        "conv_w": nrm(ks[13], (DEPTH, CONV_WIDTH, 2 * D_FF), CONV_WIDTH ** -0.5),
        "conv_b": nrm(ks[14], (DEPTH, 2 * D_FF), 0.01),
        "w_down": nrm(ks[15], (DEPTH, D_FF, D_MODEL), BETA * D_FF ** -0.5),
        "ln2_g": 1.0 + nrm(ks[16], (DEPTH, D_MODEL), 0.02),
        "ln2_b": nrm(ks[17], (DEPTH, D_MODEL), 0.02),
    }


def reference(x_prompt, x_sample, rel_bias, w_in, sink, q_norm_g, w_uq, kv_norm_g, w_ukv, w_o,
              ln1_g, ln1_b, w_up, conv_w, conv_b, w_down, ln2_g, ln2_b):
    y_prompt = run_trunk(x_prompt, rel_bias, w_in, sink, q_norm_g, w_uq, kv_norm_g, w_ukv, w_o,
                         ln1_g, ln1_b, w_up, conv_w, conv_b, w_down, ln2_g, ln2_b)
    y_sample = run_trunk(x_sample, rel_bias, w_in, sink, q_norm_g, w_uq, kv_norm_g, w_ukv, w_o,
                         ln1_g, ln1_b, w_up, conv_w, conv_b, w_down, ln2_g, ln2_b)
    return (y_prompt, y_sample)
```

```python
import functools
import math

import jax
import jax.numpy as jnp
from jax import lax
from jax.experimental import pallas as pl
from jax.experimental.pallas import tpu as pltpu

HEAD_DIM = 128
N_HEADS_A = 8
N_KV_A = 2
GROUP_A = N_HEADS_A // N_KV_A
WINDOW = 128
BLOCK = 128
N_BUCKETS = 32
MAX_DISTANCE = 128
N_HEADS_B = 8
Q_LORA = 512
KV_LORA = 256
D_NOPE = 128
D_ROPE = 64
D_V = 128
ROPE_THETA = 10000.0
LN_EPS = 1e-5
RMS_EPS = 1e-6
NEG_INF = -1e30

QA_COLS = N_HEADS_A * HEAD_DIM
KA_COLS = N_KV_A * HEAD_DIM
QK_B = 2 * HEAD_DIM

LANES = 128
V7X_VMEM_BYTES = 64 * 1024 * 1024
VMEM_LIMIT = V7X_VMEM_BYTES - 8 * 1024 * 1024

BF16 = jnp.bfloat16
F32 = jnp.float32


def _cparams(semantics):
    return pltpu.CompilerParams(dimension_semantics=semantics, vmem_limit_bytes=VMEM_LIMIT)


def _in_proj_kernel(x_ref, w_in_ref, gq_ref, gkv_ref, w_uq_ref, w_ukv_ref, cs_ref,
                    qa_ref, ka_ref, va_ref, qb_ref, kb_ref, vt_ref, *, scale_a, scale_b):
    xb = x_ref[...].astype(BF16)
    h = jnp.dot(xb, w_in_ref[...], preferred_element_type=F32)
    c0 = 0
    qa_ref[...] = (h[:, c0:c0 + QA_COLS] * scale_a).astype(BF16)
    c0 += QA_COLS
    ka_ref[...] = h[:, c0:c0 + KA_COLS].astype(BF16)
    c0 += KA_COLS
    va_ref[...] = h[:, c0:c0 + KA_COLS].astype(BF16)
    c0 += KA_COLS
    cq = h[:, c0:c0 + Q_LORA]
    c0 += Q_LORA
    ckv = h[:, c0:c0 + KV_LORA]
    c0 += KV_LORA
    kr_pair = h[:, c0:c0 + 2 * D_ROPE]

    cqn = cq * lax.rsqrt(jnp.mean(cq * cq, axis=-1, keepdims=True) + RMS_EPS) * gq_ref[...]
    ckvn = ckv * lax.rsqrt(jnp.mean(ckv * ckv, axis=-1, keepdims=True) + RMS_EPS) * gkv_ref[...]
    qb = jnp.dot(cqn.astype(BF16), w_uq_ref[...], preferred_element_type=F32)
    kvb = jnp.dot(ckvn.astype(BF16), w_ukv_ref[...], preferred_element_type=F32)

    cs = cs_ref[...]
    lane = lax.broadcasted_iota(jnp.int32, cs.shape, 1)
    kt = kr_pair * cs
    kr = kt + pltpu.roll(kt, D_ROPE, 1)
    kr = jnp.where(lane < D_ROPE, kr, 0.0).astype(BF16)
    for hd in range(N_HEADS_B):
        b0 = hd * QK_B
        qt = qb[:, b0 + D_NOPE:b0 + QK_B] * cs
        qr = qt + pltpu.roll(qt, D_ROPE, 1)
        qb_ref[hd, :, 0:D_NOPE] = (qb[:, b0:b0 + D_NOPE] * scale_b).astype(BF16)
        qb_ref[hd, :, D_NOPE:QK_B] = (qr * scale_b).astype(BF16)
        kb_ref[hd, :, 0:D_NOPE] = kvb[:, b0:b0 + D_NOPE].astype(BF16)
        kb_ref[hd, :, D_NOPE:QK_B] = kr
        vt_ref[hd] = kvb[:, b0 + D_NOPE:b0 + D_NOPE + D_V].T.astype(BF16)


def _in_proj(x, w_in, gq, gkv, w_uq, w_ukv, cs, *, ts):
    B, S, D = x.shape
    n_in = w_in.shape[1]
    scale_a = HEAD_DIM ** -0.5
    scale_b = (D_NOPE + D_ROPE) ** -0.5
    grid = (B, S // ts)
    full = lambda shape: pl.BlockSpec(shape, lambda b, i: (0,) * len(shape))
    out_shape = (
        jax.ShapeDtypeStruct((B, S, QA_COLS), BF16),
        jax.ShapeDtypeStruct((B, S, KA_COLS), BF16),
        jax.ShapeDtypeStruct((B, S, KA_COLS), BF16),
        jax.ShapeDtypeStruct((B, N_HEADS_B, S, QK_B), BF16),
        jax.ShapeDtypeStruct((B, N_HEADS_B, S, QK_B), BF16),
        jax.ShapeDtypeStruct((B, N_HEADS_B, D_V, S), BF16),
    )
    return pl.pallas_call(
        functools.partial(_in_proj_kernel, scale_a=scale_a, scale_b=scale_b),
        grid=grid,
        in_specs=[
            pl.BlockSpec((None, ts, D), lambda b, i: (b, i, 0)),
            full((D, n_in)),
            full((1, Q_LORA)),
            full((1, KV_LORA)),
            full(w_uq.shape),
            full(w_ukv.shape),
            pl.BlockSpec((ts, LANES), lambda b, i: (i, 0)),
        ],
        out_specs=(
            pl.BlockSpec((None, ts, QA_COLS), lambda b, i: (b, i, 0)),
            pl.BlockSpec((None, ts, KA_COLS), lambda b, i: (b, i, 0)),
            pl.BlockSpec((None, ts, KA_COLS), lambda b, i: (b, i, 0)),
            pl.BlockSpec((None, N_HEADS_B, ts, QK_B), lambda b, i: (b, 0, i, 0)),
            pl.BlockSpec((None, N_HEADS_B, ts, QK_B), lambda b, i: (b, 0, i, 0)),
            pl.BlockSpec((None, N_HEADS_B, D_V, ts), lambda b, i: (b, 0, 0, i)),
        ),
        out_shape=out_shape,
        compiler_params=_cparams(("parallel", "parallel")),
        name="in_proj",
    )(x, w_in, gq, gkv, w_uq, w_ukv, cs)


def _swa_kernel(sink_ref, q_ref, k_ref, v_ref, bias_ref, o_ref, *, tq):
    S = k_ref.shape[0]
    nb = S // BLOCK
    qi = pl.program_id(1)
    for j in range(tq // BLOCK):
        n = qi * (tq // BLOCK) + j
        case = jnp.where(n == 0, 0, jnp.where(n == nb - 1, 2, 1))
        r_prev = pl.multiple_of(jnp.maximum(n - 1, 0) * BLOCK, BLOCK)
        r_cur = pl.multiple_of(n * BLOCK, BLOCK)
        r_next = pl.multiple_of(jnp.minimum(n + 1, nb - 1) * BLOCK, BLOCK)
        for g in range(N_KV_A):
            cols = slice(g * HEAD_DIM, (g + 1) * HEAD_DIM)
            kw = jnp.concatenate([k_ref[pl.ds(r_prev, BLOCK), cols], k_ref[pl.ds(r_cur, BLOCK), cols],
                                  k_ref[pl.ds(r_next, BLOCK), cols]], axis=0)
            vw = jnp.concatenate([v_ref[pl.ds(r_prev, BLOCK), cols], v_ref[pl.ds(r_cur, BLOCK), cols],
                                  v_ref[pl.ds(r_next, BLOCK), cols]], axis=0)
            heads = range(g * GROUP_A, (g + 1) * GROUP_A)
            qs = jnp.concatenate([q_ref[j * BLOCK:(j + 1) * BLOCK, hd * HEAD_DIM:(hd + 1) * HEAD_DIM]
                                  for hd in heads], axis=0)
            s_all = lax.dot_general(qs, kw, (((1,), (1,)), ((), ())), preferred_element_type=F32)
            ps = []
            inv_ls = []
            for t, hd in enumerate(heads):
                s = s_all[t * BLOCK:(t + 1) * BLOCK] + bias_ref[case, hd]
                sk = sink_ref[hd]
                m = jnp.maximum(jnp.max(s, axis=-1, keepdims=True), sk)
                p = jnp.exp(s - m)
                l = jnp.sum(p, axis=-1, keepdims=True) + jnp.exp(sk - m)
                ps.append(p.astype(BF16))
                inv_ls.append(1.0 / l)
            o_all = jnp.dot(jnp.concatenate(ps, axis=0), vw, preferred_element_type=F32)
            for t, hd in enumerate(heads):
                o = o_all[t * BLOCK:(t + 1) * BLOCK] * inv_ls[t]
                o_ref[j * BLOCK:(j + 1) * BLOCK, hd * HEAD_DIM:(hd + 1) * HEAD_DIM] = o.astype(o_ref.dtype)


def _swa(sink, qa, ka, va, bias, *, tq):
    B, S, _ = qa.shape
    grid = (B, S // tq)
    return pl.pallas_call(
        functools.partial(_swa_kernel, tq=tq),
        grid=grid,
        in_specs=[
            pl.BlockSpec(memory_space=pltpu.SMEM),
            pl.BlockSpec((None, tq, QA_COLS), lambda b, i: (b, i, 0)),
            pl.BlockSpec((None, S, KA_COLS), lambda b, i: (b, 0, 0)),
            pl.BlockSpec((None, S, KA_COLS), lambda b, i: (b, 0, 0)),
            pl.BlockSpec(bias.shape, lambda b, i: (0, 0, 0, 0)),
        ],
        out_specs=pl.BlockSpec((None, tq, QA_COLS), lambda b, i: (b, i, 0)),
        out_shape=jax.ShapeDtypeStruct((B, S, QA_COLS), BF16),
        compiler_params=_cparams(("parallel", "parallel")),
        name="swa",
    )(sink, qa, ka, va, bias)


def _mla_kernel(q_ref, k_ref, vt_ref, o_ref, s_scr, acc_scr, *, ck):
    S = k_ref.shape[0]
    tq = q_ref.shape[0]
    nck = S // ck
    q = q_ref[...]

    def scores(c, m):
        r0 = pl.multiple_of(c * ck, ck)
        s = lax.dot_general(k_ref[pl.ds(r0, ck), :], q, (((1,), (1,)), ((), ())),
                            preferred_element_type=F32)
        s_scr[pl.ds(r0, ck), :] = s
        return jnp.maximum(m, jnp.max(s, axis=0, keepdims=True))

    m = lax.fori_loop(0, nck, scores, jnp.full((1, tq), -jnp.inf, F32))

    acc_scr[...] = jnp.zeros_like(acc_scr)

    def weighted(c, l):
        r0 = pl.multiple_of(c * ck, ck)
        p = jnp.exp(s_scr[pl.ds(r0, ck), :] - m)
        acc_scr[...] += jnp.dot(vt_ref[:, pl.ds(r0, ck)], p.astype(BF16), preferred_element_type=F32)
        return l + jnp.sum(p, axis=0, keepdims=True)

    l = lax.fori_loop(0, nck, weighted, jnp.zeros((1, tq), F32))
    o = acc_scr[...] * (1.0 / l)
    o_ref[...] = o.T.astype(o_ref.dtype)


def _mla(qb, kb, vt, *, tq, ck):
    B, H, S, _ = qb.shape
    grid = (B, H, S // tq)
    return pl.pallas_call(
        functools.partial(_mla_kernel, ck=ck),
        grid=grid,
        in_specs=[
            pl.BlockSpec((None, None, tq, QK_B), lambda b, h, i: (b, h, i, 0)),
            pl.BlockSpec((None, None, S, QK_B), lambda b, h, i: (b, h, 0, 0)),
            pl.BlockSpec((None, None, D_V, S), lambda b, h, i: (b, h, 0, 0)),
        ],
        out_specs=pl.BlockSpec((None, tq, D_V), lambda b, h, i: (b, i, h)),
        out_shape=jax.ShapeDtypeStruct((B, S, H * D_V), BF16),
        scratch_shapes=[pltpu.VMEM((S, tq), F32), pltpu.VMEM((D_V, tq), F32)],
        compiler_params=_cparams(("parallel", "parallel", "parallel")),
        name="mla",
    )(qb, kb, vt)


def _layer_norm(y, g, b):
    mu = jnp.mean(y, axis=-1, keepdims=True)
    d = y - mu
    var = jnp.mean(d * d, axis=-1, keepdims=True)
    return d * lax.rsqrt(var + LN_EPS) * g + b


def _out_ln_kernel(oa_ref, ob_ref, x_ref, woa_ref, wob_ref, g_ref, b_ref, o_ref, *, alpha):
    attn = jnp.dot(oa_ref[...], woa_ref[...], preferred_element_type=F32)
    attn += jnp.dot(ob_ref[...], wob_ref[...], preferred_element_type=F32)
    y = alpha * x_ref[...] + attn
    o_ref[...] = _layer_norm(y, g_ref[...], b_ref[...])


def _out_ln(oa, ob, x, woa, wob, g, b, *, alpha, ts):
    B, S, D = x.shape
    grid = (B, S // ts)
    full = lambda shape: pl.BlockSpec(shape, lambda b_, i: (0,) * len(shape))
    return pl.pallas_call(
        functools.partial(_out_ln_kernel, alpha=alpha),
        grid=grid,
        in_specs=[
            pl.BlockSpec((None, ts, oa.shape[-1]), lambda b_, i: (b_, i, 0)),
            pl.BlockSpec((None, ts, ob.shape[-1]), lambda b_, i: (b_, i, 0)),
            pl.BlockSpec((None, ts, D), lambda b_, i: (b_, i, 0)),
            full(woa.shape),
            full(wob.shape),
            full((1, D)),
            full((1, D)),
        ],
        out_specs=pl.BlockSpec((None, ts, D), lambda b_, i: (b_, i, 0)),
        out_shape=jax.ShapeDtypeStruct((B, S, D), F32),
        compiler_params=_cparams(("parallel", "parallel")),
        name="out_ln",
    )(oa, ob, x, woa, wob, g, b)


HALO = 8


def _ffn_kernel(xp_ref, x_ref, xn_ref, wg_ref, wv_ref, cwg_ref, cwv_ref, cbg_ref, cbv_ref, wd_ref,
                g_ref, b_ref, o_ref, xb_scr, acc_scr, *, alpha):
    i = pl.program_id(1)
    c = pl.program_id(2)
    nt = pl.num_programs(1)
    nc = pl.num_programs(2)
    tm = x_ref.shape[0]

    @pl.when(c == 0)
    def _():
        xb_scr[0:tm, :] = x_ref[...].astype(BF16)
        xb_scr[tm:tm + 2 * HALO, :] = jnp.concatenate([xp_ref[...], xn_ref[...]], axis=0).astype(BF16)

    xb = xb_scr[...]
    row = lax.broadcasted_iota(jnp.int32, (tm, 1), 0)
    has_prev = (i > 0).astype(F32)
    has_next = (i < nt - 1).astype(F32)

    def conv_branch(w_ref, cw_ref, cb_ref):
        u_ext = jnp.dot(xb, w_ref[...], preferred_element_type=F32)
        u = u_ext[0:tm]
        u_prev = u_ext[tm + HALO - 1:tm + HALO] * has_prev
        u_next = u_ext[tm + HALO:tm + HALO + 1] * has_next
        u_dn = jnp.where(row == 0, u_prev, pltpu.roll(u, 1, 0))
        u_up = jnp.where(row == tm - 1, u_next, pltpu.roll(u, tm - 1, 0))
        cw = cw_ref[...]
        return u_dn * cw[0:1] + u * cw[1:2] + u_up * cw[2:3] + cb_ref[...]

    gate = conv_branch(wg_ref, cwg_ref, cbg_ref)
    val = conv_branch(wv_ref, cwv_ref, cbv_ref)
    act = (gate * (1.0 / (1.0 + jnp.exp(-gate))) * val).astype(BF16)
    part = jnp.dot(act, wd_ref[...], preferred_element_type=F32)

    @pl.when(c == 0)
    def _():
        acc_scr[...] = part

    @pl.when(c > 0)
    def _():
        acc_scr[...] += part

    @pl.when(c == nc - 1)
    def _():
        y = alpha * x_ref[...] + acc_scr[...]
        o_ref[...] = _layer_norm(y, g_ref[...], b_ref[...])


def _ffn(x, w_up, conv_w, conv_b, w_down, g, b, *, alpha, tm, ck):
    B, S, D = x.shape
    F = w_down.shape[0]
    nck = F // ck
    nh = tm // HALO
    grid = (B, S // tm, nck)
    last_halo = S // HALO - 1
    return pl.pallas_call(
        functools.partial(_ffn_kernel, alpha=alpha),
        grid=grid,
        in_specs=[
            pl.BlockSpec((None, HALO, D), lambda b_, i, c: (b_, jnp.maximum(i * nh - 1, 0), 0)),
            pl.BlockSpec((None, tm, D), lambda b_, i, c: (b_, i, 0)),
            pl.BlockSpec((None, HALO, D), lambda b_, i, c: (b_, jnp.minimum((i + 1) * nh, last_halo), 0)),
            pl.BlockSpec((D, ck), lambda b_, i, c: (0, c)),
            pl.BlockSpec((D, ck), lambda b_, i, c: (0, nck + c)),
            pl.BlockSpec((3, ck), lambda b_, i, c: (0, c)),
            pl.BlockSpec((3, ck), lambda b_, i, c: (0, nck + c)),
            pl.BlockSpec((1, ck), lambda b_, i, c: (0, c)),
            pl.BlockSpec((1, ck), lambda b_, i, c: (0, nck + c)),
            pl.BlockSpec((ck, D), lambda b_, i, c: (c, 0)),
            pl.BlockSpec((1, D), lambda b_, i, c: (0, 0)),
            pl.BlockSpec((1, D), lambda b_, i, c: (0, 0)),
        ],
        out_specs=pl.BlockSpec((None, tm, D), lambda b_, i, c: (b_, i, 0)),
        out_shape=jax.ShapeDtypeStruct((B, S, D), F32),
        scratch_shapes=[pltpu.VMEM((tm + 2 * HALO, D), BF16), pltpu.VMEM((tm, D), F32)],
        compiler_params=_cparams(("parallel", "parallel", "arbitrary")),
        name="ffn",
    )(x, x, x, w_up, w_up, conv_w, conv_w, conv_b, conv_b, w_down, g, b)


def _t5_bucket(rel):
    half = N_BUCKETS // 2
    max_exact = half // 2
    ret = (rel > 0).astype(jnp.int32) * half
    n = jnp.abs(rel)
    large = max_exact + (jnp.log(jnp.maximum(n, 1).astype(jnp.float32) / max_exact)
                         / math.log(MAX_DISTANCE / max_exact) * (half - max_exact)).astype(jnp.int32)
    large = jnp.minimum(large, half - 1)
    return ret + jnp.where(n < max_exact, n, large)


def _window_bias(rel_bias):
    qi = jnp.arange(BLOCK, dtype=jnp.int32)
    ki = jnp.arange(3 * BLOCK, dtype=jnp.int32)
    rel = ki[None, :] - BLOCK - qi[:, None]
    bias = rel_bias.astype(F32)[_t5_bucket(rel)].transpose(2, 0, 1)
    base = jnp.where((jnp.abs(rel) <= WINDOW)[None], bias, NEG_INF)
    first = jnp.where((ki >= BLOCK)[None, None, :], base, NEG_INF)
    last = jnp.where((ki < 2 * BLOCK)[None, None, :], base, NEG_INF)
    return jnp.stack([first, base, last])


def _rope_table(S):
    inv = 1.0 / (ROPE_THETA ** (jnp.arange(0, D_ROPE, 2, dtype=F32) / D_ROPE))
    ang = jnp.arange(S, dtype=F32)[:, None] * inv[None, :]
    cos, sin = jnp.cos(ang), jnp.sin(ang)
    return jnp.concatenate([cos, cos, -sin, sin], axis=-1)


def _swap_halves(w):
    return jnp.concatenate([w[..., D_ROPE // 2:], w[..., :D_ROPE // 2]], axis=-1)


def _tiles(S):
    mla_elems = 4 * 1024 * 1024
    tq_mla = max(256, min(S, mla_elems // S))
    return dict(ts=256, tq_swa=512, tq_mla=tq_mla, ck_mla=512, ts_out=512, tm_ffn=512, ck_ffn=512)


def kernel(x_prompt, x_sample, rel_bias, w_in, sink, q_norm_g, w_uq, kv_norm_g, w_ukv, w_o,
           ln1_g, ln1_b, w_up, conv_w, conv_b, w_down, ln2_g, ln2_b):
    depth = w_in.shape[0]
    alpha = (2 * depth) ** 0.25
    in_cols = w_in.shape[-1]
    w_in_x = jnp.concatenate([w_in, _swap_halves(w_in[..., in_cols - D_ROPE:])], axis=-1).astype(BF16)
    wq = w_uq.reshape(depth, Q_LORA, N_HEADS_B, D_NOPE + D_ROPE)
    w_uq_x = jnp.concatenate([wq, _swap_halves(wq[..., D_NOPE:])], axis=-1)
    w_uq_x = w_uq_x.reshape(depth, Q_LORA, N_HEADS_B * QK_B).astype(BF16)
    w_ukv_b = w_ukv.astype(BF16)
    w_oa = w_o[:, :QA_COLS].astype(BF16)
    w_ob = w_o[:, QA_COLS:].astype(BF16)
    w_up_b = w_up.astype(BF16)
    w_down_b = w_down.astype(BF16)
    bias = _window_bias(rel_bias)

    def trunk(x):
        B, S, D = x.shape
        t = _tiles(S)
        cs = _rope_table(S)
        for l in range(depth):
            qa, ka, va, qb, kb, vt = _in_proj(
                x, w_in_x[l], q_norm_g[l][None], kv_norm_g[l][None], w_uq_x[l], w_ukv_b[l], cs, ts=t["ts"])
            oa = _swa(sink[l], qa, ka, va, bias, tq=t["tq_swa"])
            ob = _mla(qb, kb, vt, tq=t["tq_mla"], ck=t["ck_mla"])
            x = _out_ln(oa, ob, x, w_oa[l], w_ob[l], ln1_g[l][None], ln1_b[l][None], alpha=alpha, ts=t["ts_out"])
            x = _ffn(x, w_up_b[l], conv_w[l], conv_b[l][None], w_down_b[l], ln2_g[l][None], ln2_b[l][None],
                     alpha=alpha, tm=t["tm_ffn"], ck=t["ck_ffn"])
        return x

    return trunk(x_prompt), trunk(x_sample)
```

```python
import functools
import math

import jax
import jax.numpy as jnp
from jax import lax
from jax.experimental import pallas as pl
from jax.experimental.pallas import tpu as pltpu

HEAD_DIM = 128
N_HEADS_A = 8
N_KV_A = 2
GROUP_A = N_HEADS_A // N_KV_A
WINDOW = 128
BLOCK = 128
N_BUCKETS = 32
MAX_DISTANCE = 128
N_HEADS_B = 8
Q_LORA = 512
KV_LORA = 256
D_NOPE = 128
D_ROPE = 64
D_V = 128
ROPE_THETA = 10000.0
LN_EPS = 1e-5
RMS_EPS = 1e-6
NEG_INF = -1e30

QA_COLS = N_HEADS_A * HEAD_DIM
KA_COLS = N_KV_A * HEAD_DIM
QK_B = D_NOPE + 2 * D_ROPE
KV_B = D_NOPE + D_V

LANES = 128
V7X_VMEM_BYTES = 64 * 1024 * 1024
VMEM_LIMIT = V7X_VMEM_BYTES - 8 * 1024 * 1024

BF16 = jnp.bfloat16
F32 = jnp.float32


def _cparams(semantics):
    return pltpu.CompilerParams(dimension_semantics=semantics, vmem_limit_bytes=VMEM_LIMIT)


def _layer_spec(arr, layer, n_grid):
    zeros = (0,) * (arr.ndim - 1)
    index_map = {2: lambda a, b: (layer,) + zeros, 3: lambda a, b, c: (layer,) + zeros}[n_grid]
    return pl.BlockSpec((None,) + arr.shape[1:], index_map, pipeline_mode=pl.Buffered(1))


def _in_proj_kernel(x_ref, w_in_ref, gq_ref, gkv_ref, w_uq_ref, w_ukv_ref, cs_ref,
                    qa_ref, ka_ref, va_ref, qb_ref, kb_ref, vt_ref, *, scale_a, scale_b):
    xb = x_ref[...].astype(BF16)
    h = jnp.dot(xb, w_in_ref[...], preferred_element_type=F32)
    c0 = 0
    qa_ref[...] = (h[:, c0:c0 + QA_COLS] * scale_a).astype(BF16)
    c0 += QA_COLS
    ka_ref[...] = h[:, c0:c0 + KA_COLS].astype(BF16)
    c0 += KA_COLS
    va_ref[...] = h[:, c0:c0 + KA_COLS].astype(BF16)
    c0 += KA_COLS
    cq = h[:, c0:c0 + Q_LORA]
    c0 += Q_LORA
    ckv = h[:, c0:c0 + KV_LORA]
    c0 += KV_LORA
    kr_pair = h[:, c0:c0 + 2 * D_ROPE]

    cqn = cq * lax.rsqrt(jnp.mean(cq * cq, axis=-1, keepdims=True) + RMS_EPS) * gq_ref[...]
    ckvn = ckv * lax.rsqrt(jnp.mean(ckv * ckv, axis=-1, keepdims=True) + RMS_EPS) * gkv_ref[...]
    qb = jnp.dot(cqn.astype(BF16), w_uq_ref[...], preferred_element_type=F32)
    kvb = jnp.dot(ckvn.astype(BF16), w_ukv_ref[...], preferred_element_type=F32)

    cs = cs_ref[...]
    lane = lax.broadcasted_iota(jnp.int32, cs.shape, 1)
    kt = kr_pair * cs
    kr = kt + pltpu.roll(kt, D_ROPE, 1)
    kr = jnp.where(lane < D_ROPE, kr, 0.0).astype(BF16)
    for hd in range(N_HEADS_B):
        q0 = hd * QK_B
        k0 = hd * KV_B
        qt = qb[:, q0 + D_NOPE:q0 + QK_B] * cs
        qr = qt + pltpu.roll(qt, D_ROPE, 1)
        qb_ref[hd, :, 0:D_NOPE] = (qb[:, q0:q0 + D_NOPE] * scale_b).astype(BF16)
        qb_ref[hd, :, D_NOPE:QK_B] = (qr * scale_b).astype(BF16)
        kb_ref[hd, :, 0:D_NOPE] = kvb[:, k0:k0 + D_NOPE].astype(BF16)
        kb_ref[hd, :, D_NOPE:QK_B] = kr
        vt_ref[hd] = kvb[:, k0 + D_NOPE:k0 + KV_B].T.astype(BF16)


def _in_proj(x, w_in, gq, gkv, w_uq, w_ukv, cs, *, layer, ts):
    B, S, D = x.shape
    scale_a = HEAD_DIM ** -0.5
    scale_b = (D_NOPE + D_ROPE) ** -0.5 * math.log2(math.e)
    grid = (B, S // ts)
    out_shape = (
        jax.ShapeDtypeStruct((B, S, QA_COLS), BF16),
        jax.ShapeDtypeStruct((B, S, KA_COLS), BF16),
        jax.ShapeDtypeStruct((B, S, KA_COLS), BF16),
        jax.ShapeDtypeStruct((B, N_HEADS_B, S, QK_B), BF16),
        jax.ShapeDtypeStruct((B, N_HEADS_B, S, QK_B), BF16),
        jax.ShapeDtypeStruct((B, N_HEADS_B, D_V, S), BF16),
    )
    return pl.pallas_call(
        functools.partial(_in_proj_kernel, scale_a=scale_a, scale_b=scale_b),
        grid=grid,
        in_specs=[
            pl.BlockSpec((None, ts, D), lambda b, i: (b, i, 0)),
            _layer_spec(w_in, layer, 2),
            _layer_spec(gq, layer, 2),
            _layer_spec(gkv, layer, 2),
            _layer_spec(w_uq, layer, 2),
            _layer_spec(w_ukv, layer, 2),
            pl.BlockSpec((ts, LANES), lambda b, i: (i, 0)),
        ],
        out_specs=(
            pl.BlockSpec((None, ts, QA_COLS), lambda b, i: (b, i, 0)),
            pl.BlockSpec((None, ts, KA_COLS), lambda b, i: (b, i, 0)),
            pl.BlockSpec((None, ts, KA_COLS), lambda b, i: (b, i, 0)),
            pl.BlockSpec((None, N_HEADS_B, ts, QK_B), lambda b, i: (b, 0, i, 0)),
            pl.BlockSpec((None, N_HEADS_B, ts, QK_B), lambda b, i: (b, 0, i, 0)),
            pl.BlockSpec((None, N_HEADS_B, D_V, ts), lambda b, i: (b, 0, 0, i)),
        ),
        out_shape=out_shape,
        compiler_params=_cparams(("parallel", "parallel")),
        name="in_proj",
    )(x, w_in, gq, gkv, w_uq, w_ukv, cs)


def _swa_kernel(sink_ref, q_ref, k_ref, v_ref, bias_ref, o_ref, *, layer, tq):
    S = k_ref.shape[0]
    nb = S // BLOCK
    qi = pl.program_id(1)
    for j in range(tq // BLOCK):
        n = qi * (tq // BLOCK) + j
        case = jnp.where(n == 0, 0, jnp.where(n == nb - 1, 2, 1))
        r_prev = pl.multiple_of(jnp.maximum(n - 1, 0) * BLOCK, BLOCK)
        r_cur = pl.multiple_of(n * BLOCK, BLOCK)
        r_next = pl.multiple_of(jnp.minimum(n + 1, nb - 1) * BLOCK, BLOCK)
        for g in range(N_KV_A):
            cols = slice(g * HEAD_DIM, (g + 1) * HEAD_DIM)
            kw = jnp.concatenate([k_ref[pl.ds(r_prev, BLOCK), cols], k_ref[pl.ds(r_cur, BLOCK), cols],
                                  k_ref[pl.ds(r_next, BLOCK), cols]], axis=0)
            vw = jnp.concatenate([v_ref[pl.ds(r_prev, BLOCK), cols], v_ref[pl.ds(r_cur, BLOCK), cols],
                                  v_ref[pl.ds(r_next, BLOCK), cols]], axis=0)
            heads = range(g * GROUP_A, (g + 1) * GROUP_A)
            qs = jnp.concatenate([q_ref[j * BLOCK:(j + 1) * BLOCK, hd * HEAD_DIM:(hd + 1) * HEAD_DIM]
                                  for hd in heads], axis=0)
            s_all = lax.dot_general(qs, kw, (((1,), (1,)), ((), ())), preferred_element_type=F32)
            ps = []
            inv_ls = []
            for t, hd in enumerate(heads):
                s = s_all[t * BLOCK:(t + 1) * BLOCK] + bias_ref[case, hd]
                sk = sink_ref[layer, hd]
                m = jnp.maximum(jnp.max(s, axis=-1, keepdims=True), sk)
                p = jnp.exp(s - m)
                l = jnp.sum(p, axis=-1, keepdims=True) + jnp.exp(sk - m)
                ps.append(p.astype(BF16))
                inv_ls.append(1.0 / l)
            o_all = jnp.dot(jnp.concatenate(ps, axis=0), vw, preferred_element_type=F32)
            for t, hd in enumerate(heads):
                o = o_all[t * BLOCK:(t + 1) * BLOCK] * inv_ls[t]
                o_ref[j * BLOCK:(j + 1) * BLOCK, hd * HEAD_DIM:(hd + 1) * HEAD_DIM] = o.astype(o_ref.dtype)


def _swa(sink, qa, ka, va, bias, *, layer, tq):
    B, S, _ = qa.shape
    grid = (B, S // tq)
    return pl.pallas_call(
        functools.partial(_swa_kernel, layer=layer, tq=tq),
        grid=grid,
        in_specs=[
            pl.BlockSpec(memory_space=pltpu.SMEM),
            pl.BlockSpec((None, tq, QA_COLS), lambda b, i: (b, i, 0)),
            pl.BlockSpec((None, S, KA_COLS), lambda b, i: (b, 0, 0)),
            pl.BlockSpec((None, S, KA_COLS), lambda b, i: (b, 0, 0)),
            pl.BlockSpec(bias.shape, lambda b, i: (0, 0, 0, 0)),
        ],
        out_specs=pl.BlockSpec((None, tq, QA_COLS), lambda b, i: (b, i, 0)),
        out_shape=jax.ShapeDtypeStruct((B, S, QA_COLS), BF16),
        compiler_params=_cparams(("parallel", "parallel")),
        name="swa",
    )(sink, qa, ka, va, bias)


def _mla_kernel(q_ref, k_ref, vt_ref, o_ref, *, ck):
    S = k_ref.shape[0]
    nck = S // ck
    q = q_ref[...]

    def scores(c):
        return lax.dot_general(k_ref[c * ck:(c + 1) * ck, :], q, (((1,), (1,)), ((), ())),
                               preferred_element_type=F32)

    s = scores(0)
    m = l = acc = None
    for c in range(nck):
        s_next = scores(c + 1) if c + 1 < nck else None
        cmax = jnp.max(s, axis=0, keepdims=True)
        m_new = cmax if c == 0 else jnp.maximum(m, cmax)
        p = jnp.exp2(s - m_new)
        psum = jnp.sum(p, axis=0, keepdims=True)
        pv = jnp.dot(vt_ref[:, c * ck:(c + 1) * ck], p.astype(BF16), preferred_element_type=F32)
        if c == 0:
            l, acc = psum, pv
        else:
            rescale = jnp.exp2(m - m_new)
            l = rescale * l + psum
            acc = rescale * acc + pv
        m, s = m_new, s_next
    o = acc * (1.0 / l)
    o_ref[...] = o.T.astype(o_ref.dtype)


def _mla(qb, kb, vt, *, tq, ck):
    B, H, S, _ = qb.shape
    grid = (B, H, S // tq)
    return pl.pallas_call(
        functools.partial(_mla_kernel, ck=ck),
        grid=grid,
        in_specs=[
            pl.BlockSpec((None, None, tq, QK_B), lambda b, h, i: (b, h, i, 0)),
            pl.BlockSpec((None, None, S, QK_B), lambda b, h, i: (b, h, 0, 0)),
            pl.BlockSpec((None, None, D_V, S), lambda b, h, i: (b, h, 0, 0)),
        ],
        out_specs=pl.BlockSpec((None, tq, D_V), lambda b, h, i: (b, i, h)),
        out_shape=jax.ShapeDtypeStruct((B, S, H * D_V), BF16),
        compiler_params=_cparams(("parallel", "parallel", "parallel")),
        name="mla",
    )(qb, kb, vt)


def _layer_norm(y, g, b):
    mu = jnp.mean(y, axis=-1, keepdims=True)
    d = y - mu
    var = jnp.mean(d * d, axis=-1, keepdims=True)
    return d * lax.rsqrt(var + LN_EPS) * g + b


def _out_ln_kernel(oa_ref, ob_ref, x_ref, woa_ref, wob_ref, g_ref, b_ref, o_ref, *, alpha):
    attn = jnp.dot(oa_ref[...], woa_ref[...], preferred_element_type=F32)
    attn += jnp.dot(ob_ref[...], wob_ref[...], preferred_element_type=F32)
    y = alpha * x_ref[...] + attn
    o_ref[...] = _layer_norm(y, g_ref[...], b_ref[...])


def _out_ln(oa, ob, x, w_o, g, b, *, layer, alpha, ts):
    B, S, D = x.shape
    na, nb = oa.shape[-1], ob.shape[-1]
    assert na == nb and w_o.shape[1] == na + nb
    grid = (B, S // ts)
    return pl.pallas_call(
        functools.partial(_out_ln_kernel, alpha=alpha),
        grid=grid,
        in_specs=[
            pl.BlockSpec((None, ts, na), lambda b_, i: (b_, i, 0)),
            pl.BlockSpec((None, ts, nb), lambda b_, i: (b_, i, 0)),
            pl.BlockSpec((None, ts, D), lambda b_, i: (b_, i, 0)),
            pl.BlockSpec((None, na, D), lambda b_, i: (layer, 0, 0), pipeline_mode=pl.Buffered(1)),
            pl.BlockSpec((None, nb, D), lambda b_, i: (layer, 1, 0), pipeline_mode=pl.Buffered(1)),
            _layer_spec(g, layer, 2),
            _layer_spec(b, layer, 2),
        ],
        out_specs=pl.BlockSpec((None, ts, D), lambda b_, i: (b_, i, 0)),
        out_shape=jax.ShapeDtypeStruct((B, S, D), F32),
        compiler_params=_cparams(("parallel", "parallel")),
        name="out_ln",
    )(oa, ob, x, w_o, w_o, g, b)


HALO = 8


def _ffn_kernel(xp_ref, x_ref, xn_ref, wg_ref, wv_ref, cwg_ref, cwv_ref, cbg_ref, cbv_ref, wd_ref,
                g_ref, b_ref, o_ref, xb_scr, act_scr, *, alpha, sub):
    i = pl.program_id(1)
    c = pl.program_id(2)
    nt = pl.num_programs(1)
    nck = pl.num_programs(2) - 1
    tm = x_ref.shape[0]
    ck = wg_ref.shape[1]

    def up_phase():
        xb = xb_scr[...]
        row = lax.broadcasted_iota(jnp.int32, (tm, 1), 0)
        has_prev = (i > 0).astype(F32)
        has_next = (i < nt - 1).astype(F32)
        slot = c % 2

        def conv_branch(w_ref, cw_ref, cb_ref, cols):
            u_ext = jnp.dot(xb, w_ref[:, cols], preferred_element_type=F32)
            u = u_ext[0:tm]
            u_prev = u_ext[tm + HALO - 1:tm + HALO] * has_prev
            u_next = u_ext[tm + HALO:tm + HALO + 1] * has_next
            u_dn = jnp.where(row == 0, u_prev, pltpu.roll(u, 1, 0))
            u_up = jnp.where(row == tm - 1, u_next, pltpu.roll(u, tm - 1, 0))
            return u_dn * cw_ref[0:1, cols] + u * cw_ref[1:2, cols] + u_up * cw_ref[2:3, cols] + cb_ref[:, cols]

        for j in range(ck // sub):
            cols = slice(j * sub, (j + 1) * sub)
            gate = conv_branch(wg_ref, cwg_ref, cbg_ref, cols)
            val = conv_branch(wv_ref, cwv_ref, cbv_ref, cols)
            act_scr[slot, :, cols] = (gate * (1.0 / (1.0 + jnp.exp(-gate))) * val).astype(BF16)

    def down_part():
        return jnp.dot(act_scr[(c + 1) % 2], wd_ref[...], preferred_element_type=F32)

    @pl.when(c == 0)
    def _():
        xb_scr[0:tm, :] = x_ref[...].astype(BF16)
        xb_scr[tm:tm + 2 * HALO, :] = jnp.concatenate([xp_ref[...], xn_ref[...]], axis=0).astype(BF16)
        o_ref[...] = jnp.zeros_like(o_ref)
        up_phase()

    @pl.when(jnp.logical_and(c > 0, c < nck))
    def _():
        up_phase()
        o_ref[...] += down_part()

    @pl.when(c == nck)
    def _():
        y = alpha * x_ref[...] + (o_ref[...] + down_part())
        o_ref[...] = _layer_norm(y, g_ref[...], b_ref[...])


def _ffn(x, w_up, conv_w, conv_b, w_down, g, b, *, layer, alpha, tm, ck, sub):
    B, S, D = x.shape
    F = w_down.shape[1]
    nck = F // ck
    nh = tm // HALO
    grid = (B, S // tm, nck + 1)
    last_halo = S // HALO - 1
    up_idx = lambda c: jnp.minimum(c, nck - 1)
    dn_idx = lambda c: jnp.maximum(c - 1, 0)
    return pl.pallas_call(
        functools.partial(_ffn_kernel, alpha=alpha, sub=sub),
        grid=grid,
        in_specs=[
            pl.BlockSpec((None, HALO, D), lambda b_, i, c: (b_, jnp.maximum(i * nh - 1, 0), 0)),
            pl.BlockSpec((None, tm, D), lambda b_, i, c: (b_, i, 0)),
            pl.BlockSpec((None, HALO, D), lambda b_, i, c: (b_, jnp.minimum((i + 1) * nh, last_halo), 0)),
            pl.BlockSpec((None, D, ck), lambda b_, i, c: (layer, 0, up_idx(c))),
            pl.BlockSpec((None, D, ck), lambda b_, i, c: (layer, 0, nck + up_idx(c))),
            pl.BlockSpec((None, 3, ck), lambda b_, i, c: (layer, 0, up_idx(c))),
            pl.BlockSpec((None, 3, ck), lambda b_, i, c: (layer, 0, nck + up_idx(c))),
            pl.BlockSpec((None, 1, ck), lambda b_, i, c: (layer, 0, up_idx(c))),
            pl.BlockSpec((None, 1, ck), lambda b_, i, c: (layer, 0, nck + up_idx(c))),
            pl.BlockSpec((None, ck, D), lambda b_, i, c: (layer, dn_idx(c), 0)),
            _layer_spec(g, layer, 3),
            _layer_spec(b, layer, 3),
        ],
        out_specs=pl.BlockSpec((None, tm, D), lambda b_, i, c: (b_, i, 0)),
        out_shape=jax.ShapeDtypeStruct((B, S, D), F32),
        scratch_shapes=[pltpu.VMEM((tm + 2 * HALO, D), BF16), pltpu.VMEM((2, tm, ck), BF16)],
        compiler_params=_cparams(("parallel", "parallel", "arbitrary")),
        name="ffn",
    )(x, x, x, w_up, w_up, conv_w, conv_w, conv_b, conv_b, w_down, g, b)


def _t5_bucket(rel):
    half = N_BUCKETS // 2
    max_exact = half // 2
    ret = (rel > 0).astype(jnp.int32) * half
    n = jnp.abs(rel)
    large = max_exact + (jnp.log(jnp.maximum(n, 1).astype(jnp.float32) / max_exact)
                         / math.log(MAX_DISTANCE / max_exact) * (half - max_exact)).astype(jnp.int32)
    large = jnp.minimum(large, half - 1)
    return ret + jnp.where(n < max_exact, n, large)


def _window_bias(rel_bias):
    qi = jnp.arange(BLOCK, dtype=jnp.int32)
    ki = jnp.arange(3 * BLOCK, dtype=jnp.int32)
    rel = ki[None, :] - BLOCK - qi[:, None]
    bucket = _t5_bucket(rel)
    table = rel_bias.astype(F32).T
    hit = bucket[None, :, :, None] == jnp.arange(N_BUCKETS, dtype=jnp.int32)
    bias = jnp.sum(jnp.where(hit, table[:, None, None, :], 0.0), axis=-1)
    base = jnp.where((jnp.abs(rel) <= WINDOW)[None], bias, NEG_INF)
    first = jnp.where((ki >= BLOCK)[None, None, :], base, NEG_INF)
    last = jnp.where((ki < 2 * BLOCK)[None, None, :], base, NEG_INF)
    return jnp.stack([first, base, last])


def _rope_table(S):
    inv = 1.0 / (ROPE_THETA ** (jnp.arange(0, D_ROPE, 2, dtype=F32) / D_ROPE))
    ang = jnp.arange(S, dtype=F32)[:, None] * inv[None, :]
    cos, sin = jnp.cos(ang), jnp.sin(ang)
    return jnp.concatenate([cos, cos, -sin, sin], axis=-1)


def _swap_halves(w):
    return jnp.concatenate([w[..., D_ROPE // 2:], w[..., :D_ROPE // 2]], axis=-1)


def _tiles(S):
    mla_elems = 4 * 1024 * 1024
    tq_mla = max(256, min(S, mla_elems // S))
    ck_mla = S // 4
    return dict(ts=256, tq_swa=512, tq_mla=tq_mla, ck_mla=ck_mla, ts_out=512, tm_ffn=512, ck_ffn=512, sub_ffn=256)


def kernel(x_prompt, x_sample, rel_bias, w_in, sink, q_norm_g, w_uq, kv_norm_g, w_ukv, w_o,
           ln1_g, ln1_b, w_up, conv_w, conv_b, w_down, ln2_g, ln2_b):
    depth = w_in.shape[0]
    alpha = (2 * depth) ** 0.25
    in_cols = w_in.shape[-1]
    w_in_x = jnp.concatenate([w_in, _swap_halves(w_in[..., in_cols - D_ROPE:])], axis=-1).astype(BF16)
    wq = w_uq.reshape(depth, Q_LORA, N_HEADS_B, D_NOPE + D_ROPE)
    w_uq_x = jnp.concatenate([wq, _swap_halves(wq[..., D_NOPE:])], axis=-1)
    w_uq_x = w_uq_x.reshape(depth, Q_LORA, N_HEADS_B * QK_B).astype(BF16)
    w_ukv_b = w_ukv.astype(BF16)
    w_o_b = w_o.astype(BF16)
    w_up_b = w_up.astype(BF16)
    w_down_b = w_down.astype(BF16)
    row = lambda p: p[:, None, :]
    gq, gkv = row(q_norm_g), row(kv_norm_g)
    g1, b1, g2, b2 = row(ln1_g), row(ln1_b), row(ln2_g), row(ln2_b)
    cb = row(conv_b)
    bias = _window_bias(rel_bias)

    def trunk(x):
        S = x.shape[1]
        t = _tiles(S)
        cs = _rope_table(S)
        for l in range(depth):
            qa, ka, va, qb, kb, vt = _in_proj(x, w_in_x, gq, gkv, w_uq_x, w_ukv_b, cs, layer=l, ts=t["ts"])
            oa = _swa(sink, qa, ka, va, bias, layer=l, tq=t["tq_swa"])
            ob = _mla(qb, kb, vt, tq=t["tq_mla"], ck=t["ck_mla"])
            x = _out_ln(oa, ob, x, w_o_b, g1, b1, layer=l, alpha=alpha, ts=t["ts_out"])
            x = _ffn(x, w_up_b, conv_w, cb, w_down_b, g2, b2, layer=l, alpha=alpha,
                     tm=t["tm_ffn"], ck=t["ck_ffn"], sub=t["sub_ffn"])
        return x

    return trunk(x_prompt), trunk(x_sample)
```

```python
import functools
import math

import jax
import jax.numpy as jnp
from jax import lax
from jax.experimental import pallas as pl
from jax.experimental.pallas import tpu as pltpu

HEAD_DIM = 128
N_HEADS_A = 8
N_KV_A = 2
GROUP_A = N_HEADS_A // N_KV_A
WINDOW = 128
BLOCK = 128
N_BUCKETS = 32
MAX_DISTANCE = 128
N_HEADS_B = 8
Q_LORA = 512
KV_LORA = 256
D_NOPE = 128
D_ROPE = 64
D_V = 128
ROPE_THETA = 10000.0
LN_EPS = 1e-5
RMS_EPS = 1e-6
NEG_INF = -1e30
LOG2E = math.log2(math.e)

QA_COLS = N_HEADS_A * HEAD_DIM
KA_COLS = N_KV_A * HEAD_DIM
QK_B = D_NOPE + 2 * D_ROPE
KV_B = D_NOPE + D_V

LANES = 128
V7X_VMEM_BYTES = 64 * 1024 * 1024
VMEM_LIMIT = V7X_VMEM_BYTES - 8 * 1024 * 1024

BF16 = jnp.bfloat16
F32 = jnp.float32


def _cparams(semantics):
    return pltpu.CompilerParams(dimension_semantics=semantics, vmem_limit_bytes=VMEM_LIMIT)


def _layer_spec(arr, layer, n_grid):
    zeros = (0,) * (arr.ndim - 1)
    index_map = {2: lambda a, b: (layer,) + zeros, 3: lambda a, b, c: (layer,) + zeros}[n_grid]
    return pl.BlockSpec((None,) + arr.shape[1:], index_map, pipeline_mode=pl.Buffered(1))


def _in_proj_kernel(x_ref, w_in_ref, gq_ref, gkv_ref, w_uq_ref, w_ukv_ref, cs_ref,
                    qa_ref, ka_ref, vat_ref, qb_ref, kb_ref, vt_ref, *, scale_a, scale_b):
    xb = x_ref[...].astype(BF16)
    h = jnp.dot(xb, w_in_ref[...], preferred_element_type=F32)
    c0 = 0
    qa_ref[...] = (h[:, c0:c0 + QA_COLS] * scale_a).astype(BF16)
    c0 += QA_COLS
    ka_ref[...] = h[:, c0:c0 + KA_COLS].astype(BF16)
    c0 += KA_COLS
    for g in range(N_KV_A):
        vat_ref[g] = h[:, c0 + g * HEAD_DIM:c0 + (g + 1) * HEAD_DIM].T.astype(BF16)
    c0 += KA_COLS
    cq = h[:, c0:c0 + Q_LORA]
    c0 += Q_LORA
    ckv = h[:, c0:c0 + KV_LORA]
    c0 += KV_LORA
    kr_pair = h[:, c0:c0 + 2 * D_ROPE]

    cqn = cq * lax.rsqrt(jnp.mean(cq * cq, axis=-1, keepdims=True) + RMS_EPS) * gq_ref[...]
    ckvn = ckv * lax.rsqrt(jnp.mean(ckv * ckv, axis=-1, keepdims=True) + RMS_EPS) * gkv_ref[...]
    qb = jnp.dot(cqn.astype(BF16), w_uq_ref[...], preferred_element_type=F32)
    kvb = jnp.dot(ckvn.astype(BF16), w_ukv_ref[...], preferred_element_type=F32)

    cs = cs_ref[...]
    lane = lax.broadcasted_iota(jnp.int32, cs.shape, 1)
    kt = kr_pair * cs
    kr = kt + pltpu.roll(kt, D_ROPE, 1)
    kr = jnp.where(lane < D_ROPE, kr, 0.0).astype(BF16)
    for hd in range(N_HEADS_B):
        q0 = hd * QK_B
        k0 = hd * KV_B
        qt = qb[:, q0 + D_NOPE:q0 + QK_B] * cs
        qr = qt + pltpu.roll(qt, D_ROPE, 1)
        qb_ref[hd, :, 0:D_NOPE] = (qb[:, q0:q0 + D_NOPE] * scale_b).astype(BF16)
        qb_ref[hd, :, D_NOPE:QK_B] = (qr * scale_b).astype(BF16)
        kb_ref[hd, :, 0:D_NOPE] = kvb[:, k0:k0 + D_NOPE].astype(BF16)
        kb_ref[hd, :, D_NOPE:QK_B] = kr
        vt_ref[hd] = kvb[:, k0 + D_NOPE:k0 + KV_B].T.astype(BF16)


def _in_proj(x, w_in, gq, gkv, w_uq, w_ukv, cs, *, layer, ts):
    B, S, D = x.shape
    scale_a = HEAD_DIM ** -0.5 * LOG2E
    scale_b = (D_NOPE + D_ROPE) ** -0.5 * LOG2E
    grid = (B, S // ts)
    out_shape = (
        jax.ShapeDtypeStruct((B, S, QA_COLS), BF16),
        jax.ShapeDtypeStruct((B, S, KA_COLS), BF16),
        jax.ShapeDtypeStruct((B, N_KV_A, HEAD_DIM, S), BF16),
        jax.ShapeDtypeStruct((B, N_HEADS_B, S, QK_B), BF16),
        jax.ShapeDtypeStruct((B, N_HEADS_B, S, QK_B), BF16),
        jax.ShapeDtypeStruct((B, N_HEADS_B, D_V, S), BF16),
    )
    return pl.pallas_call(
        functools.partial(_in_proj_kernel, scale_a=scale_a, scale_b=scale_b),
        grid=grid,
        in_specs=[
            pl.BlockSpec((None, ts, D), lambda b, i: (b, i, 0)),
            _layer_spec(w_in, layer, 2),
            _layer_spec(gq, layer, 2),
            _layer_spec(gkv, layer, 2),
            _layer_spec(w_uq, layer, 2),
            _layer_spec(w_ukv, layer, 2),
            pl.BlockSpec((ts, LANES), lambda b, i: (i, 0)),
        ],
        out_specs=(
            pl.BlockSpec((None, ts, QA_COLS), lambda b, i: (b, i, 0)),
            pl.BlockSpec((None, ts, KA_COLS), lambda b, i: (b, i, 0)),
            pl.BlockSpec((None, N_KV_A, HEAD_DIM, ts), lambda b, i: (b, 0, 0, i)),
            pl.BlockSpec((None, N_HEADS_B, ts, QK_B), lambda b, i: (b, 0, i, 0)),
            pl.BlockSpec((None, N_HEADS_B, ts, QK_B), lambda b, i: (b, 0, i, 0)),
            pl.BlockSpec((None, N_HEADS_B, D_V, ts), lambda b, i: (b, 0, 0, i)),
        ),
        out_shape=out_shape,
        compiler_params=_cparams(("parallel", "parallel")),
        name="in_proj",
    )(x, w_in, gq, gkv, w_uq, w_ukv, cs)


def _swa_kernel(sink_ref, q_ref, k_ref, vt_ref, bias_ref, o_ref, *, layer, tq):
    S = k_ref.shape[0]
    nb = S // BLOCK
    qi = pl.program_id(1)
    blocks = [(j, g) for j in range(tq // BLOCK) for g in range(N_KV_A)]

    def window(j):
        n = qi * (tq // BLOCK) + j
        case = jnp.where(n == 0, 0, jnp.where(n == nb - 1, 2, 1))
        rows = [pl.multiple_of(jnp.maximum(n - 1, 0) * BLOCK, BLOCK), pl.multiple_of(n * BLOCK, BLOCK),
                pl.multiple_of(jnp.minimum(n + 1, nb - 1) * BLOCK, BLOCK)]
        return case, rows

    def scores(j, g):
        case, rows = window(j)
        cols = slice(g * HEAD_DIM, (g + 1) * HEAD_DIM)
        kw = jnp.concatenate([k_ref[pl.ds(r, BLOCK), cols] for r in rows], axis=0)
        qs = jnp.concatenate([q_ref[j * BLOCK:(j + 1) * BLOCK, hd * HEAD_DIM:(hd + 1) * HEAD_DIM]
                              for hd in range(g * GROUP_A, (g + 1) * GROUP_A)], axis=0)
        s = lax.dot_general(kw, qs, (((1,), (1,)), ((), ())), preferred_element_type=F32)
        return s + bias_ref[case, g]

    s_next = scores(*blocks[0])
    for idx, (j, g) in enumerate(blocks):
        s = s_next
        if idx + 1 < len(blocks):
            s_next = scores(*blocks[idx + 1])
        _, rows = window(j)
        heads = range(g * GROUP_A, (g + 1) * GROUP_A)
        vwt = jnp.concatenate([vt_ref[g, :, pl.ds(r, BLOCK)] for r in rows], axis=1)
        sink = jnp.concatenate([jnp.full((1, BLOCK), sink_ref[layer, hd] * LOG2E, F32) for hd in heads], axis=1)
        m = jnp.maximum(jnp.max(s, axis=0, keepdims=True), sink)
        p = jnp.exp2(s - m)
        l = jnp.sum(p, axis=0, keepdims=True) + jnp.exp2(sink - m)
        o = jnp.dot(vwt, p.astype(BF16), preferred_element_type=F32) * (1.0 / l)
        for t, hd in enumerate(heads):
            o_ref[j * BLOCK:(j + 1) * BLOCK, hd * HEAD_DIM:(hd + 1) * HEAD_DIM] = (
                o[:, t * BLOCK:(t + 1) * BLOCK].T.astype(o_ref.dtype))


def _swa(sink, qa, ka, vat, bias, *, layer, tq):
    B, S, _ = qa.shape
    grid = (B, S // tq)
    return pl.pallas_call(
        functools.partial(_swa_kernel, layer=layer, tq=tq),
        grid=grid,
        in_specs=[
            pl.BlockSpec(memory_space=pltpu.SMEM),
            pl.BlockSpec((None, tq, QA_COLS), lambda b, i: (b, i, 0)),
            pl.BlockSpec((None, S, KA_COLS), lambda b, i: (b, 0, 0)),
            pl.BlockSpec((None, N_KV_A, HEAD_DIM, S), lambda b, i: (b, 0, 0, 0)),
            pl.BlockSpec(bias.shape, lambda b, i: (0, 0, 0, 0)),
        ],
        out_specs=pl.BlockSpec((None, tq, QA_COLS), lambda b, i: (b, i, 0)),
        out_shape=jax.ShapeDtypeStruct((B, S, QA_COLS), BF16),
        compiler_params=_cparams(("parallel", "parallel")),
        name="swa",
    )(sink, qa, ka, vat, bias)


def _mla_kernel(q_ref, k_ref, vt_ref, o_ref, *, ck):
    S = k_ref.shape[0]
    nck = S // ck
    q = q_ref[...]

    def scores(c):
        return lax.dot_general(k_ref[c * ck:(c + 1) * ck, :], q, (((1,), (1,)), ((), ())),
                               preferred_element_type=F32)

    s = scores(0)
    m = l = acc = None
    for c in range(nck):
        s_next = scores(c + 1) if c + 1 < nck else None
        cmax = jnp.max(s, axis=0, keepdims=True)
        m_new = cmax if c == 0 else jnp.maximum(m, cmax)
        p = jnp.exp2(s - m_new)
        psum = jnp.sum(p, axis=0, keepdims=True)
        pv = jnp.dot(vt_ref[:, c * ck:(c + 1) * ck], p.astype(BF16), preferred_element_type=F32)
        if c == 0:
            l, acc = psum, pv
        else:
            rescale = jnp.exp2(m - m_new)
            l = rescale * l + psum
            acc = rescale * acc + pv
        m, s = m_new, s_next
    o = acc * (1.0 / l)
    o_ref[...] = o.T.astype(o_ref.dtype)


def _mla(qb, kb, vt, *, tq, ck):
    B, H, S, _ = qb.shape
    grid = (B, H, S // tq)
    return pl.pallas_call(
        functools.partial(_mla_kernel, ck=ck),
        grid=grid,
        in_specs=[
            pl.BlockSpec((None, None, tq, QK_B), lambda b, h, i: (b, h, i, 0)),
            pl.BlockSpec((None, None, S, QK_B), lambda b, h, i: (b, h, 0, 0)),
            pl.BlockSpec((None, None, D_V, S), lambda b, h, i: (b, h, 0, 0)),
        ],
        out_specs=pl.BlockSpec((None, tq, D_V), lambda b, h, i: (b, i, h)),
        out_shape=jax.ShapeDtypeStruct((B, S, H * D_V), BF16),
        compiler_params=_cparams(("parallel", "parallel", "parallel")),
        name="mla",
    )(qb, kb, vt)


def _layer_norm(y, g, b):
    mu = jnp.mean(y, axis=-1, keepdims=True)
    d = y - mu
    var = jnp.mean(d * d, axis=-1, keepdims=True)
    return d * lax.rsqrt(var + LN_EPS) * g + b


def _out_ln_kernel(oa_ref, ob_ref, x_ref, wo_ref, g_ref, b_ref, o_ref, *, alpha, n_sub):
    ts = x_ref.shape[0]
    rs = ts // n_sub
    for r in range(n_sub):
        rows = slice(r * rs, (r + 1) * rs)
        o_cat = jnp.concatenate([oa_ref[rows, :], ob_ref[rows, :]], axis=-1)
        attn = jnp.dot(o_cat, wo_ref[...], preferred_element_type=F32)
        y = alpha * x_ref[rows, :] + attn
        o_ref[rows, :] = _layer_norm(y, g_ref[...], b_ref[...])


def _out_ln(oa, ob, x, w_o, g, b, *, layer, alpha, ts, n_sub):
    B, S, D = x.shape
    na, nb = oa.shape[-1], ob.shape[-1]
    grid = (B, S // ts)
    return pl.pallas_call(
        functools.partial(_out_ln_kernel, alpha=alpha, n_sub=n_sub),
        grid=grid,
        in_specs=[
            pl.BlockSpec((None, ts, na), lambda b_, i: (b_, i, 0)),
            pl.BlockSpec((None, ts, nb), lambda b_, i: (b_, i, 0)),
            pl.BlockSpec((None, ts, D), lambda b_, i: (b_, i, 0)),
            _layer_spec(w_o, layer, 2),
            _layer_spec(g, layer, 2),
            _layer_spec(b, layer, 2),
        ],
        out_specs=pl.BlockSpec((None, ts, D), lambda b_, i: (b_, i, 0)),
        out_shape=jax.ShapeDtypeStruct((B, S, D), F32),
        compiler_params=_cparams(("parallel", "parallel")),
        name="out_ln",
    )(oa, ob, x, w_o, g, b)


HALO = 8


def _ffn_kernel(xp_ref, x_ref, xn_ref, wu_ref, cw_ref, cb_ref, wd_ref, g_ref, b_ref, o_ref,
                xb_scr, act_scr, *, alpha, sub):
    i = pl.program_id(1)
    c = pl.program_id(2)
    nt = pl.num_programs(1)
    nck = pl.num_programs(2) - 1
    tm = x_ref.shape[0]
    ck = wd_ref.shape[0]

    def up_phase():
        xb = xb_scr[...]
        row = lax.broadcasted_iota(jnp.int32, (tm, 1), 0)
        has_prev = (i > 0).astype(F32)
        has_next = (i < nt - 1).astype(F32)
        slot = c % 2

        def conv_branch(cols):
            u_ext = jnp.dot(xb, wu_ref[:, cols], preferred_element_type=F32)
            u = u_ext[0:tm]
            u_prev = u_ext[tm + HALO - 1:tm + HALO] * has_prev
            u_next = u_ext[tm + HALO:tm + HALO + 1] * has_next
            u_dn = jnp.where(row == 0, u_prev, pltpu.roll(u, 1, 0))
            u_up = jnp.where(row == tm - 1, u_next, pltpu.roll(u, tm - 1, 0))
            return u_dn * cw_ref[0:1, cols] + u * cw_ref[1:2, cols] + u_up * cw_ref[2:3, cols] + cb_ref[:, cols]

        for j in range(ck // sub):
            gate = conv_branch(slice(j * sub, (j + 1) * sub))
            val = conv_branch(slice(ck + j * sub, ck + (j + 1) * sub))
            act_scr[slot, :, j * sub:(j + 1) * sub] = (gate * (1.0 / (1.0 + jnp.exp(-gate))) * val).astype(BF16)

    def down_part():
        return jnp.dot(act_scr[(c + 1) % 2], wd_ref[...], preferred_element_type=F32)

    @pl.when(c == 0)
    def _():
        xb_scr[0:tm, :] = x_ref[...].astype(BF16)
        xb_scr[tm:tm + 2 * HALO, :] = jnp.concatenate([xp_ref[...], xn_ref[...]], axis=0).astype(BF16)
        o_ref[...] = jnp.zeros_like(o_ref)
        up_phase()

    @pl.when(jnp.logical_and(c > 0, c < nck))
    def _():
        up_phase()
        o_ref[...] += down_part()

    @pl.when(c == nck)
    def _():
        y = alpha * x_ref[...] + (o_ref[...] + down_part())
        o_ref[...] = _layer_norm(y, g_ref[...], b_ref[...])


def _ffn(x, w_up, conv_w, conv_b, w_down, g, b, *, layer, alpha, tm, sub):
    B, S, D = x.shape
    nck, ck = w_up.shape[1], w_up.shape[3] // 2
    nh = tm // HALO
    grid = (B, S // tm, nck + 1)
    last_halo = S // HALO - 1
    up_idx = lambda c: jnp.minimum(c, nck - 1)
    dn_idx = lambda c: jnp.maximum(c - 1, 0)
    return pl.pallas_call(
        functools.partial(_ffn_kernel, alpha=alpha, sub=sub),
        grid=grid,
        in_specs=[
            pl.BlockSpec((None, HALO, D), lambda b_, i, c: (b_, jnp.maximum(i * nh - 1, 0), 0)),
            pl.BlockSpec((None, tm, D), lambda b_, i, c: (b_, i, 0), pipeline_mode=pl.Buffered(1)),
            pl.BlockSpec((None, HALO, D), lambda b_, i, c: (b_, jnp.minimum((i + 1) * nh, last_halo), 0)),
            pl.BlockSpec((None, None, D, 2 * ck), lambda b_, i, c: (layer, up_idx(c), 0, 0)),
            pl.BlockSpec((None, None, 3, 2 * ck), lambda b_, i, c: (layer, up_idx(c), 0, 0)),
            pl.BlockSpec((None, None, 1, 2 * ck), lambda b_, i, c: (layer, up_idx(c), 0, 0)),
            pl.BlockSpec((None, ck, D), lambda b_, i, c: (layer, dn_idx(c), 0)),
            _layer_spec(g, layer, 3),
            _layer_spec(b, layer, 3),
        ],
        out_specs=pl.BlockSpec((None, tm, D), lambda b_, i, c: (b_, i, 0)),
        out_shape=jax.ShapeDtypeStruct((B, S, D), F32),
        scratch_shapes=[pltpu.VMEM((tm + 2 * HALO, D), BF16), pltpu.VMEM((2, tm, ck), BF16)],
        compiler_params=_cparams(("parallel", "parallel", "arbitrary")),
        name="ffn",
    )(x, x, x, w_up, conv_w, conv_b, w_down, g, b)


def _t5_bucket(rel):
    half = N_BUCKETS // 2
    max_exact = half // 2
    ret = (rel > 0).astype(jnp.int32) * half
    n = jnp.abs(rel)
    large = max_exact + (jnp.log(jnp.maximum(n, 1).astype(jnp.float32) / max_exact)
                         / math.log(MAX_DISTANCE / max_exact) * (half - max_exact)).astype(jnp.int32)
    large = jnp.minimum(large, half - 1)
    return ret + jnp.where(n < max_exact, n, large)


def _window_bias(rel_bias):
    qi = jnp.arange(BLOCK, dtype=jnp.int32)
    ki = jnp.arange(3 * BLOCK, dtype=jnp.int32)
    rel = ki[None, :] - BLOCK - qi[:, None]
    bucket = _t5_bucket(rel)
    table = rel_bias.astype(F32).T
    hit = bucket[None, :, :, None] == jnp.arange(N_BUCKETS, dtype=jnp.int32)
    bias = jnp.sum(jnp.where(hit, table[:, None, None, :], 0.0), axis=-1)
    base = jnp.where((jnp.abs(rel) <= WINDOW)[None], bias, NEG_INF)
    first = jnp.where((ki >= BLOCK)[None, None, :], base, NEG_INF)
    last = jnp.where((ki < 2 * BLOCK)[None, None, :], base, NEG_INF)
    tables = jnp.stack([first, base, last]) * LOG2E
    tables = tables.reshape(3, N_KV_A, GROUP_A, BLOCK, 3 * BLOCK).transpose(0, 1, 4, 2, 3)
    return tables.reshape(3, N_KV_A, 3 * BLOCK, GROUP_A * BLOCK)


def _rope_table(S):
    inv = 1.0 / (ROPE_THETA ** (jnp.arange(0, D_ROPE, 2, dtype=F32) / D_ROPE))
    ang = jnp.arange(S, dtype=F32)[:, None] * inv[None, :]
    cos, sin = jnp.cos(ang), jnp.sin(ang)
    return jnp.concatenate([cos, cos, -sin, sin], axis=-1)


def _swap_halves(w):
    return jnp.concatenate([w[..., D_ROPE // 2:], w[..., :D_ROPE // 2]], axis=-1)


def _chunk_major(w, ck):
    depth, r, two_f = w.shape
    nck = two_f // 2 // ck
    w = w.reshape(depth, r, 2, nck, ck)
    return w.transpose(0, 3, 1, 2, 4).reshape(depth, nck, r, 2 * ck)


def _tiles(S):
    mla_elems = 4 * 1024 * 1024
    tq_mla = max(256, min(S, mla_elems // S))
    ck_mla = S // 4
    return dict(ts=256, tq_swa=512, tq_mla=tq_mla, ck_mla=ck_mla, ts_out=512, sub_out=2,
                tm_ffn=min(1024, S), ck_ffn=512, sub_ffn=256)


def kernel(x_prompt, x_sample, rel_bias, w_in, sink, q_norm_g, w_uq, kv_norm_g, w_ukv, w_o,
           ln1_g, ln1_b, w_up, conv_w, conv_b, w_down, ln2_g, ln2_b):
    depth = w_in.shape[0]
    alpha = (2 * depth) ** 0.25
    in_cols = w_in.shape[-1]
    w_in_x = jnp.concatenate([w_in, _swap_halves(w_in[..., in_cols - D_ROPE:])], axis=-1).astype(BF16)
    wq = w_uq.reshape(depth, Q_LORA, N_HEADS_B, D_NOPE + D_ROPE)
    w_uq_x = jnp.concatenate([wq, _swap_halves(wq[..., D_NOPE:])], axis=-1)
    w_uq_x = w_uq_x.reshape(depth, Q_LORA, N_HEADS_B * QK_B).astype(BF16)
    w_ukv_b = w_ukv.astype(BF16)
    w_o_b = w_o.astype(BF16)
    ck_ffn = _tiles(x_prompt.shape[1])["ck_ffn"]
    w_up_b = _chunk_major(w_up, ck_ffn).astype(BF16)
    conv_w_c = _chunk_major(conv_w, ck_ffn)
    w_down_b = w_down.astype(BF16)
    row = lambda p: p[:, None, :]
    gq, gkv = row(q_norm_g), row(kv_norm_g)
    g1, b1, g2, b2 = row(ln1_g), row(ln1_b), row(ln2_g), row(ln2_b)
    cb = _chunk_major(row(conv_b), ck_ffn)
    bias = _window_bias(rel_bias)

    def trunk(x):
        S = x.shape[1]
        t = _tiles(S)
        cs = _rope_table(S)
        for l in range(depth):
            qa, ka, vat, qb, kb, vt = _in_proj(x, w_in_x, gq, gkv, w_uq_x, w_ukv_b, cs, layer=l, ts=t["ts"])
            oa = _swa(sink, qa, ka, vat, bias, layer=l, tq=t["tq_swa"])
            ob = _mla(qb, kb, vt, tq=t["tq_mla"], ck=t["ck_mla"])
            x = _out_ln(oa, ob, x, w_o_b, g1, b1, layer=l, alpha=alpha, ts=t["ts_out"], n_sub=t["sub_out"])
            x = _ffn(x, w_up_b, conv_w_c, cb, w_down_b, g2, b2, layer=l, alpha=alpha,
                     tm=t["tm_ffn"], sub=t["sub_ffn"])
        return x

    return trunk(x_prompt), trunk(x_sample)
```

```python
import functools
import math

import jax
import jax.numpy as jnp
from jax import lax
from jax.experimental import pallas as pl
from jax.experimental.pallas import tpu as pltpu

HEAD_DIM = 128
N_HEADS_A = 8
N_KV_A = 2
GROUP_A = N_HEADS_A // N_KV_A
WINDOW = 128
BLOCK = 128
N_BUCKETS = 32
MAX_DISTANCE = 128
N_HEADS_B = 8
Q_LORA = 512
KV_LORA = 256
D_NOPE = 128
D_ROPE = 64
D_V = 128
ROPE_THETA = 10000.0
LN_EPS = 1e-5
RMS_EPS = 1e-6
NEG_INF = -1e30
LOG2E = math.log2(math.e)

QA_COLS = N_HEADS_A * HEAD_DIM
KA_COLS = N_KV_A * HEAD_DIM
QK_B = D_NOPE + 2 * D_ROPE
KV_B = D_NOPE + D_V

LANES = 128
V7X_VMEM_BYTES = 64 * 1024 * 1024
VMEM_LIMIT = V7X_VMEM_BYTES - 8 * 1024 * 1024

BF16 = jnp.bfloat16
F32 = jnp.float32


def _cparams(semantics):
    return pltpu.CompilerParams(dimension_semantics=semantics, vmem_limit_bytes=VMEM_LIMIT)


def _layer_spec(arr, layer, n_grid):
    zeros = (0,) * (arr.ndim - 1)
    index_map = {2: lambda a, b: (layer,) + zeros, 3: lambda a, b, c: (layer,) + zeros}[n_grid]
    return pl.BlockSpec((None,) + arr.shape[1:], index_map, pipeline_mode=pl.Buffered(1))


def _in_proj_kernel(x_ref, w_in_ref, gq_ref, gkv_ref, w_uq_ref, w_ukv_ref, cs_ref,
                    qa_ref, ka_ref, vat_ref, qb_ref, kb_ref, vt_ref, *, scale_a, scale_b):
    xb = x_ref[...].astype(BF16)
    h = jnp.dot(xb, w_in_ref[...], preferred_element_type=F32)
    c0 = 0
    qa_ref[...] = (h[:, c0:c0 + QA_COLS] * scale_a).astype(BF16)
    c0 += QA_COLS
    ka_ref[...] = h[:, c0:c0 + KA_COLS].astype(BF16)
    c0 += KA_COLS
    for g in range(N_KV_A):
        vat_ref[g] = h[:, c0 + g * HEAD_DIM:c0 + (g + 1) * HEAD_DIM].T.astype(BF16)
    c0 += KA_COLS
    cq = h[:, c0:c0 + Q_LORA]
    c0 += Q_LORA
    ckv = h[:, c0:c0 + KV_LORA]
    c0 += KV_LORA
    kr_pair = h[:, c0:c0 + 2 * D_ROPE]

    cqn = cq * lax.rsqrt(jnp.mean(cq * cq, axis=-1, keepdims=True) + RMS_EPS) * gq_ref[...]
    ckvn = ckv * lax.rsqrt(jnp.mean(ckv * ckv, axis=-1, keepdims=True) + RMS_EPS) * gkv_ref[...]
    qb = jnp.dot(cqn.astype(BF16), w_uq_ref[...], preferred_element_type=F32)
    kvb = jnp.dot(ckvn.astype(BF16), w_ukv_ref[...], preferred_element_type=F32)

    cs = cs_ref[...]
    lane = lax.broadcasted_iota(jnp.int32, cs.shape, 1)
    kt = kr_pair * cs
    kr = kt + pltpu.roll(kt, D_ROPE, 1)
    kr = jnp.where(lane < D_ROPE, kr, 0.0).astype(BF16)
    for hd in range(N_HEADS_B):
        q0 = hd * QK_B
        k0 = hd * KV_B
        qt = qb[:, q0 + D_NOPE:q0 + QK_B] * cs
        qr = qt + pltpu.roll(qt, D_ROPE, 1)
        qb_ref[hd, :, 0:D_NOPE] = (qb[:, q0:q0 + D_NOPE] * scale_b).astype(BF16)
        qb_ref[hd, :, D_NOPE:QK_B] = (qr * scale_b).astype(BF16)
        kb_ref[hd, :, 0:D_NOPE] = kvb[:, k0:k0 + D_NOPE].astype(BF16)
        kb_ref[hd, :, D_NOPE:QK_B] = kr
        vt_ref[hd] = kvb[:, k0 + D_NOPE:k0 + KV_B].T.astype(BF16)


def _in_proj(x, w_in, gq, gkv, w_uq, w_ukv, cs, *, layer, ts):
    B, S, D = x.shape
    scale_a = HEAD_DIM ** -0.5 * LOG2E
    scale_b = (D_NOPE + D_ROPE) ** -0.5 * LOG2E
    grid = (B, S // ts)
    out_shape = (
        jax.ShapeDtypeStruct((B, S, QA_COLS), BF16),
        jax.ShapeDtypeStruct((B, S, KA_COLS), BF16),
        jax.ShapeDtypeStruct((B, N_KV_A, HEAD_DIM, S), BF16),
        jax.ShapeDtypeStruct((B, N_HEADS_B, S, QK_B), BF16),
        jax.ShapeDtypeStruct((B, N_HEADS_B, S, QK_B), BF16),
        jax.ShapeDtypeStruct((B, N_HEADS_B, D_V, S), BF16),
    )
    return pl.pallas_call(
        functools.partial(_in_proj_kernel, scale_a=scale_a, scale_b=scale_b),
        grid=grid,
        in_specs=[
            pl.BlockSpec((None, ts, D), lambda b, i: (b, i, 0)),
            _layer_spec(w_in, layer, 2),
            _layer_spec(gq, layer, 2),
            _layer_spec(gkv, layer, 2),
            _layer_spec(w_uq, layer, 2),
            _layer_spec(w_ukv, layer, 2),
            pl.BlockSpec((ts, LANES), lambda b, i: (i, 0)),
        ],
        out_specs=(
            pl.BlockSpec((None, ts, QA_COLS), lambda b, i: (b, i, 0)),
            pl.BlockSpec((None, ts, KA_COLS), lambda b, i: (b, i, 0)),
            pl.BlockSpec((None, N_KV_A, HEAD_DIM, ts), lambda b, i: (b, 0, 0, i)),
            pl.BlockSpec((None, N_HEADS_B, ts, QK_B), lambda b, i: (b, 0, i, 0)),
            pl.BlockSpec((None, N_HEADS_B, ts, QK_B), lambda b, i: (b, 0, i, 0)),
            pl.BlockSpec((None, N_HEADS_B, D_V, ts), lambda b, i: (b, 0, 0, i)),
        ),
        out_shape=out_shape,
        compiler_params=_cparams(("parallel", "parallel")),
        name="in_proj",
    )(x, w_in, gq, gkv, w_uq, w_ukv, cs)


def _swa_kernel(sink_ref, q_ref, k_ref, vt_ref, bias_ref, o_ref, *, layer, tq):
    S = k_ref.shape[0]
    nb = S // BLOCK
    qi = pl.program_id(1)
    blocks = [(j, g) for j in range(tq // BLOCK) for g in range(N_KV_A)]

    def window(j):
        n = qi * (tq // BLOCK) + j
        case = jnp.where(n == 0, 0, jnp.where(n == nb - 1, 2, 1))
        rows = [pl.multiple_of(jnp.maximum(n - 1, 0) * BLOCK, BLOCK), pl.multiple_of(n * BLOCK, BLOCK),
                pl.multiple_of(jnp.minimum(n + 1, nb - 1) * BLOCK, BLOCK)]
        return case, rows

    def scores(j, g):
        case, rows = window(j)
        cols = slice(g * HEAD_DIM, (g + 1) * HEAD_DIM)
        kw = jnp.concatenate([k_ref[pl.ds(r, BLOCK), cols] for r in rows], axis=0)
        qs = jnp.concatenate([q_ref[j * BLOCK:(j + 1) * BLOCK, hd * HEAD_DIM:(hd + 1) * HEAD_DIM]
                              for hd in range(g * GROUP_A, (g + 1) * GROUP_A)], axis=0)
        s = lax.dot_general(kw, qs, (((1,), (1,)), ((), ())), preferred_element_type=F32)
        return s + bias_ref[case, g]

    s_next = scores(*blocks[0])
    for idx, (j, g) in enumerate(blocks):
        s = s_next
        if idx + 1 < len(blocks):
            s_next = scores(*blocks[idx + 1])
        _, rows = window(j)
        heads = range(g * GROUP_A, (g + 1) * GROUP_A)
        vwt = jnp.concatenate([vt_ref[g, :, pl.ds(r, BLOCK)] for r in rows], axis=1)
        sink = jnp.concatenate([jnp.full((1, BLOCK), sink_ref[layer, hd] * LOG2E, F32) for hd in heads], axis=1)
        m = jnp.maximum(jnp.max(s, axis=0, keepdims=True), sink)
        p = jnp.exp2(s - m)
        l = jnp.sum(p, axis=0, keepdims=True) + jnp.exp2(sink - m)
        o = jnp.dot(vwt, p.astype(BF16), preferred_element_type=F32) * (1.0 / l)
        for t, hd in enumerate(heads):
            o_ref[j * BLOCK:(j + 1) * BLOCK, hd * HEAD_DIM:(hd + 1) * HEAD_DIM] = (
                o[:, t * BLOCK:(t + 1) * BLOCK].T.astype(o_ref.dtype))


def _swa(sink, qa, ka, vat, bias, *, layer, tq):
    B, S, _ = qa.shape
    grid = (B, S // tq)
    return pl.pallas_call(
        functools.partial(_swa_kernel, layer=layer, tq=tq),
        grid=grid,
        in_specs=[
            pl.BlockSpec(memory_space=pltpu.SMEM),
            pl.BlockSpec((None, tq, QA_COLS), lambda b, i: (b, i, 0)),
            pl.BlockSpec((None, S, KA_COLS), lambda b, i: (b, 0, 0)),
            pl.BlockSpec((None, N_KV_A, HEAD_DIM, S), lambda b, i: (b, 0, 0, 0)),
            pl.BlockSpec(bias.shape, lambda b, i: (0, 0, 0, 0)),
        ],
        out_specs=pl.BlockSpec((None, tq, QA_COLS), lambda b, i: (b, i, 0)),
        out_shape=jax.ShapeDtypeStruct((B, S, QA_COLS), BF16),
        compiler_params=_cparams(("parallel", "parallel")),
        name="swa",
    )(sink, qa, ka, vat, bias)


def _mla_kernel(q_ref, k_ref, vt_ref, o_ref, *, ck):
    S = k_ref.shape[0]
    nck = S // ck
    q = q_ref[...]

    def scores(c):
        return lax.dot_general(k_ref[c * ck:(c + 1) * ck, :], q, (((1,), (1,)), ((), ())),
                               preferred_element_type=F32)

    s = scores(0)
    m = l = acc = None
    for c in range(nck):
        s_next = scores(c + 1) if c + 1 < nck else None
        cmax = jnp.max(s, axis=0, keepdims=True)
        m_new = cmax if c == 0 else jnp.maximum(m, cmax)
        p = jnp.exp2(s - m_new)
        psum = jnp.sum(p, axis=0, keepdims=True)
        pv = jnp.dot(vt_ref[:, c * ck:(c + 1) * ck], p.astype(BF16), preferred_element_type=F32)
        if c == 0:
            l, acc = psum, pv
        else:
            rescale = jnp.exp2(m - m_new)
            l = rescale * l + psum
            acc = rescale * acc + pv
        m, s = m_new, s_next
    o = acc * (1.0 / l)
    o_ref[...] = o.T.astype(o_ref.dtype)


def _mla(qb, kb, vt, *, tq, ck):
    B, H, S, _ = qb.shape
    grid = (B, H, S // tq)
    return pl.pallas_call(
        functools.partial(_mla_kernel, ck=ck),
        grid=grid,
        in_specs=[
            pl.BlockSpec((None, None, tq, QK_B), lambda b, h, i: (b, h, i, 0)),
            pl.BlockSpec((None, None, S, QK_B), lambda b, h, i: (b, h, 0, 0)),
            pl.BlockSpec((None, None, D_V, S), lambda b, h, i: (b, h, 0, 0)),
        ],
        out_specs=pl.BlockSpec((None, tq, D_V), lambda b, h, i: (b, i, h)),
        out_shape=jax.ShapeDtypeStruct((B, S, H * D_V), BF16),
        compiler_params=_cparams(("parallel", "parallel", "parallel")),
        name="mla",
    )(qb, kb, vt)


def _layer_norm(y, g, b):
    mu = jnp.mean(y, axis=-1, keepdims=True)
    d = y - mu
    var = jnp.mean(d * d, axis=-1, keepdims=True)
    return d * lax.rsqrt(var + LN_EPS) * g + b


def _out_ln_kernel(oa_ref, ob_ref, x_ref, wo_ref, g_ref, b_ref, o_ref, *, alpha, n_sub):
    ts = x_ref.shape[0]
    rs = ts // n_sub
    for r in range(n_sub):
        rows = slice(r * rs, (r + 1) * rs)
        o_cat = jnp.concatenate([oa_ref[rows, :], ob_ref[rows, :]], axis=-1)
        attn = jnp.dot(o_cat, wo_ref[...], preferred_element_type=F32)
        y = alpha * x_ref[rows, :] + attn
        o_ref[rows, :] = _layer_norm(y, g_ref[...], b_ref[...])


def _out_ln(oa, ob, x, w_o, g, b, *, layer, alpha, ts, n_sub):
    B, S, D = x.shape
    na, nb = oa.shape[-1], ob.shape[-1]
    grid = (B, S // ts)
    return pl.pallas_call(
        functools.partial(_out_ln_kernel, alpha=alpha, n_sub=n_sub),
        grid=grid,
        in_specs=[
            pl.BlockSpec((None, ts, na), lambda b_, i: (b_, i, 0)),
            pl.BlockSpec((None, ts, nb), lambda b_, i: (b_, i, 0)),
            pl.BlockSpec((None, ts, D), lambda b_, i: (b_, i, 0)),
            _layer_spec(w_o, layer, 2),
            _layer_spec(g, layer, 2),
            _layer_spec(b, layer, 2),
        ],
        out_specs=pl.BlockSpec((None, ts, D), lambda b_, i: (b_, i, 0)),
        out_shape=jax.ShapeDtypeStruct((B, S, D), F32),
        compiler_params=_cparams(("parallel", "parallel")),
        name="out_ln",
    )(oa, ob, x, w_o, g, b)


HALO = 8


def _ffn_kernel(xp_ref, x_ref, xn_ref, wg_ref, wv_ref, cwg_ref, cwv_ref, cbg_ref, cbv_ref, wd_ref,
                g_ref, b_ref, o_ref, xb_scr, act_scr, *, alpha):
    i = pl.program_id(1)
    c = pl.program_id(2)
    nt = pl.num_programs(1)
    nck = pl.num_programs(2) - 1
    tm = x_ref.shape[0]

    def up_phase():
        xb = xb_scr[...]
        row = lax.broadcasted_iota(jnp.int32, (tm, 1), 0)
        has_prev = (i > 0).astype(F32)
        has_next = (i < nt - 1).astype(F32)

        def conv_branch(w_ref, cw_ref, cb_ref):
            u_ext = jnp.dot(xb, w_ref[...], preferred_element_type=F32)
            u = u_ext[0:tm]
            u_prev = u_ext[tm + HALO - 1:tm + HALO] * has_prev
            u_next = u_ext[tm + HALO:tm + HALO + 1] * has_next
            u_dn = jnp.where(row == 0, u_prev, pltpu.roll(u, 1, 0))
            u_up = jnp.where(row == tm - 1, u_next, pltpu.roll(u, tm - 1, 0))
            return u_dn * cw_ref[0:1, :] + u * cw_ref[1:2, :] + u_up * cw_ref[2:3, :] + cb_ref[...]

        gate = conv_branch(wg_ref, cwg_ref, cbg_ref)
        val = conv_branch(wv_ref, cwv_ref, cbv_ref)
        act_scr[c % 2] = (gate * (1.0 / (1.0 + jnp.exp(-gate))) * val).astype(BF16)

    def down_part():
        return jnp.dot(act_scr[(c + 1) % 2], wd_ref[...], preferred_element_type=F32)

    @pl.when(c == 0)
    def _():
        xb_scr[0:tm, :] = x_ref[...].astype(BF16)
        xb_scr[tm:tm + 2 * HALO, :] = jnp.concatenate([xp_ref[...], xn_ref[...]], axis=0).astype(BF16)
        o_ref[...] = jnp.zeros_like(o_ref)
        up_phase()

    @pl.when(jnp.logical_and(c > 0, c < nck))
    def _():
        up_phase()
        o_ref[...] += down_part()

    @pl.when(c == nck)
    def _():
        y = alpha * x_ref[...] + (o_ref[...] + down_part())
        o_ref[...] = _layer_norm(y, g_ref[...], b_ref[...])


def _ffn(x, w_up, conv_w, conv_b, w_down, g, b, *, layer, alpha, tm, ck):
    B, S, D = x.shape
    F = w_down.shape[1]
    nck = F // ck
    nh = tm // HALO
    grid = (B, S // tm, nck + 1)
    last_halo = S // HALO - 1
    up_idx = lambda c: jnp.minimum(c, nck - 1)
    dn_idx = lambda c: jnp.maximum(c - 1, 0)
    return pl.pallas_call(
        functools.partial(_ffn_kernel, alpha=alpha),
        grid=grid,
        in_specs=[
            pl.BlockSpec((None, HALO, D), lambda b_, i, c: (b_, jnp.maximum(i * nh - 1, 0), 0)),
            pl.BlockSpec((None, tm, D), lambda b_, i, c: (b_, i, 0), pipeline_mode=pl.Buffered(1)),
            pl.BlockSpec((None, HALO, D), lambda b_, i, c: (b_, jnp.minimum((i + 1) * nh, last_halo), 0)),
            pl.BlockSpec((None, D, ck), lambda b_, i, c: (layer, 0, up_idx(c))),
            pl.BlockSpec((None, D, ck), lambda b_, i, c: (layer, 0, nck + up_idx(c))),
            pl.BlockSpec((None, 3, ck), lambda b_, i, c: (layer, 0, up_idx(c))),
            pl.BlockSpec((None, 3, ck), lambda b_, i, c: (layer, 0, nck + up_idx(c))),
            pl.BlockSpec((None, 1, ck), lambda b_, i, c: (layer, 0, up_idx(c))),
            pl.BlockSpec((None, 1, ck), lambda b_, i, c: (layer, 0, nck + up_idx(c))),
            pl.BlockSpec((None, ck, D), lambda b_, i, c: (layer, dn_idx(c), 0)),
            _layer_spec(g, layer, 3),
            _layer_spec(b, layer, 3),
        ],
        out_specs=pl.BlockSpec((None, tm, D), lambda b_, i, c: (b_, i, 0)),
        out_shape=jax.ShapeDtypeStruct((B, S, D), F32),
        scratch_shapes=[pltpu.VMEM((tm + 2 * HALO, D), BF16), pltpu.VMEM((2, tm, ck), BF16)],
        compiler_params=_cparams(("parallel", "parallel", "arbitrary")),
        name="ffn",
    )(x, x, x, w_up, w_up, conv_w, conv_w, conv_b, conv_b, w_down, g, b)


def _t5_bucket(rel):
    half = N_BUCKETS // 2
    max_exact = half // 2
    ret = (rel > 0).astype(jnp.int32) * half
    n = jnp.abs(rel)
    large = max_exact + (jnp.log(jnp.maximum(n, 1).astype(jnp.float32) / max_exact)
                         / math.log(MAX_DISTANCE / max_exact) * (half - max_exact)).astype(jnp.int32)
    large = jnp.minimum(large, half - 1)
    return ret + jnp.where(n < max_exact, n, large)


def _window_bias(rel_bias):
    qi = jnp.arange(BLOCK, dtype=jnp.int32)
    ki = jnp.arange(3 * BLOCK, dtype=jnp.int32)
    rel = ki[None, :] - BLOCK - qi[:, None]
    bucket = _t5_bucket(rel)
    table = rel_bias.astype(F32).T
    hit = bucket[None, :, :, None] == jnp.arange(N_BUCKETS, dtype=jnp.int32)
    bias = jnp.sum(jnp.where(hit, table[:, None, None, :], 0.0), axis=-1)
    base = jnp.where((jnp.abs(rel) <= WINDOW)[None], bias, NEG_INF)
    first = jnp.where((ki >= BLOCK)[None, None, :], base, NEG_INF)
    last = jnp.where((ki < 2 * BLOCK)[None, None, :], base, NEG_INF)
    tables = jnp.stack([first, base, last]) * LOG2E
    tables = tables.reshape(3, N_KV_A, GROUP_A, BLOCK, 3 * BLOCK).transpose(0, 1, 4, 2, 3)
    return tables.reshape(3, N_KV_A, 3 * BLOCK, GROUP_A * BLOCK)


def _rope_table(S):
    inv = 1.0 / (ROPE_THETA ** (jnp.arange(0, D_ROPE, 2, dtype=F32) / D_ROPE))
    ang = jnp.arange(S, dtype=F32)[:, None] * inv[None, :]
    cos, sin = jnp.cos(ang), jnp.sin(ang)
    return jnp.concatenate([cos, cos, -sin, sin], axis=-1)


def _swap_halves(w):
    return jnp.concatenate([w[..., D_ROPE // 2:], w[..., :D_ROPE // 2]], axis=-1)


def _tiles(S):
    mla_elems = 4 * 1024 * 1024
    tq_mla = max(256, min(S, mla_elems // S))
    ck_mla = S // 2
    return dict(ts=512, tq_swa=512, tq_mla=tq_mla, ck_mla=ck_mla, ts_out=512, sub_out=2,
                tm_ffn=min(1024, S), ck_ffn=512)


def kernel(x_prompt, x_sample, rel_bias, w_in, sink, q_norm_g, w_uq, kv_norm_g, w_ukv, w_o,
           ln1_g, ln1_b, w_up, conv_w, conv_b, w_down, ln2_g, ln2_b):
    depth = w_in.shape[0]
    alpha = (2 * depth) ** 0.25
    in_cols = w_in.shape[-1]
    w_in_x = jnp.concatenate([w_in, _swap_halves(w_in[..., in_cols - D_ROPE:])], axis=-1).astype(BF16)
    wq = w_uq.reshape(depth, Q_LORA, N_HEADS_B, D_NOPE + D_ROPE)
    w_uq_x = jnp.concatenate([wq, _swap_halves(wq[..., D_NOPE:])], axis=-1)
    w_uq_x = w_uq_x.reshape(depth, Q_LORA, N_HEADS_B * QK_B).astype(BF16)
    w_ukv_b = w_ukv.astype(BF16)
    w_o_b = w_o.astype(BF16)
    w_up_b = w_up.astype(BF16)
    w_down_b = w_down.astype(BF16)
    row = lambda p: p[:, None, :]
    gq, gkv = row(q_norm_g), row(kv_norm_g)
    g1, b1, g2, b2 = row(ln1_g), row(ln1_b), row(ln2_g), row(ln2_b)
    cb = row(conv_b)
    bias = _window_bias(rel_bias)

    def trunk(x):
        S = x.shape[1]
        t = _tiles(S)
        cs = _rope_table(S)
        for l in range(depth):
            qa, ka, vat, qb, kb, vt = _in_proj(x, w_in_x, gq, gkv, w_uq_x, w_ukv_b, cs, layer=l, ts=t["ts"])
            oa = _swa(sink, qa, ka, vat, bias, layer=l, tq=t["tq_swa"])
            ob = _mla(qb, kb, vt, tq=t["tq_mla"], ck=t["ck_mla"])
            x = _out_ln(oa, ob, x, w_o_b, g1, b1, layer=l, alpha=alpha, ts=t["ts_out"], n_sub=t["sub_out"])
            x = _ffn(x, w_up_b, conv_w, cb, w_down_b, g2, b2, layer=l, alpha=alpha,
                     tm=t["tm_ffn"], ck=t["ck_ffn"])
        return x

    return trunk(x_prompt), trunk(x_sample)
```

```python
import functools
import math

import jax
import jax.numpy as jnp
from jax import lax
from jax.experimental import pallas as pl
from jax.experimental.pallas import tpu as pltpu

HEAD_DIM = 128
N_HEADS_A = 8
N_KV_A = 2
GROUP_A = N_HEADS_A // N_KV_A
WINDOW = 128
BLOCK = 128
N_BUCKETS = 32
MAX_DISTANCE = 128
N_HEADS_B = 8
Q_LORA = 512
KV_LORA = 256
D_NOPE = 128
D_ROPE = 64
D_V = 128
ROPE_THETA = 10000.0
LN_EPS = 1e-5
RMS_EPS = 1e-6
NEG_INF = -1e30
LOG2E = math.log2(math.e)

QA_COLS = N_HEADS_A * HEAD_DIM
KA_COLS = N_KV_A * HEAD_DIM
QK_B = D_NOPE + 2 * D_ROPE
KV_B = D_NOPE + D_V

LANES = 128
V7X_VMEM_BYTES = 64 * 1024 * 1024
VMEM_LIMIT = V7X_VMEM_BYTES - 3 * 1024 * 1024

BF16 = jnp.bfloat16
F32 = jnp.float32


def _cparams(semantics):
    return pltpu.CompilerParams(dimension_semantics=semantics, vmem_limit_bytes=VMEM_LIMIT)


def _layer_spec(arr, layer, n_grid):
    zeros = (0,) * (arr.ndim - 1)
    index_map = {2: lambda a, b: (layer,) + zeros, 3: lambda a, b, c: (layer,) + zeros}[n_grid]
    return pl.BlockSpec((None,) + arr.shape[1:], index_map, pipeline_mode=pl.Buffered(1))


def _in_proj_kernel(x_ref, w_in_ref, gq_ref, gkv_ref, w_uq_ref, w_ukv_ref, cs_ref,
                    qa_ref, ka_ref, vat_ref, qb_ref, kb_ref, vt_ref, *, scale_a, scale_b):
    xb = x_ref[...].astype(BF16)
    h = jnp.dot(xb, w_in_ref[...], preferred_element_type=F32)
    c0 = 0
    qa_ref[...] = (h[:, c0:c0 + QA_COLS] * scale_a).astype(BF16)
    c0 += QA_COLS
    ka_ref[...] = h[:, c0:c0 + KA_COLS].astype(BF16)
    c0 += KA_COLS
    for g in range(N_KV_A):
        vat_ref[g] = h[:, c0 + g * HEAD_DIM:c0 + (g + 1) * HEAD_DIM].T.astype(BF16)
    c0 += KA_COLS
    cq = h[:, c0:c0 + Q_LORA]
    c0 += Q_LORA
    ckv = h[:, c0:c0 + KV_LORA]
    c0 += KV_LORA
    kr_pair = h[:, c0:c0 + 2 * D_ROPE]

    cqn = cq * lax.rsqrt(jnp.mean(cq * cq, axis=-1, keepdims=True) + RMS_EPS) * gq_ref[...]
    ckvn = ckv * lax.rsqrt(jnp.mean(ckv * ckv, axis=-1, keepdims=True) + RMS_EPS) * gkv_ref[...]
    qb = jnp.dot(cqn.astype(BF16), w_uq_ref[...], preferred_element_type=F32)
    kvb = jnp.dot(ckvn.astype(BF16), w_ukv_ref[...], preferred_element_type=F32)

    cs = cs_ref[...]
    lane = lax.broadcasted_iota(jnp.int32, cs.shape, 1)
    kt = kr_pair * cs
    kr = kt + pltpu.roll(kt, D_ROPE, 1)
    kr = jnp.where(lane < D_ROPE, kr, 0.0).astype(BF16)
    for hd in range(N_HEADS_B):
        q0 = hd * QK_B
        k0 = hd * KV_B
        qt = qb[:, q0 + D_NOPE:q0 + QK_B] * cs
        qr = qt + pltpu.roll(qt, D_ROPE, 1)
        qb_ref[hd, :, 0:D_NOPE] = (qb[:, q0:q0 + D_NOPE] * scale_b).astype(BF16)
        qb_ref[hd, :, D_NOPE:QK_B] = (qr * scale_b).astype(BF16)
        kb_ref[hd, :, 0:D_NOPE] = kvb[:, k0:k0 + D_NOPE].astype(BF16)
        kb_ref[hd, :, D_NOPE:QK_B] = kr
        vt_ref[hd] = kvb[:, k0 + D_NOPE:k0 + KV_B].T.astype(BF16)


def _in_proj(x, w_in, gq, gkv, w_uq, w_ukv, cs, *, layer, ts):
    B, S, D = x.shape
    scale_a = HEAD_DIM ** -0.5 * LOG2E
    scale_b = (D_NOPE + D_ROPE) ** -0.5 * LOG2E
    grid = (B, S // ts)
    out_shape = (
        jax.ShapeDtypeStruct((B, S, QA_COLS), BF16),
        jax.ShapeDtypeStruct((B, S, KA_COLS), BF16),
        jax.ShapeDtypeStruct((B, N_KV_A, HEAD_DIM, S), BF16),
        jax.ShapeDtypeStruct((B, N_HEADS_B, S, QK_B), BF16),
        jax.ShapeDtypeStruct((B, N_HEADS_B, S, QK_B), BF16),
        jax.ShapeDtypeStruct((B, N_HEADS_B, D_V, S), BF16),
    )
    return pl.pallas_call(
        functools.partial(_in_proj_kernel, scale_a=scale_a, scale_b=scale_b),
        grid=grid,
        in_specs=[
            pl.BlockSpec((None, ts, D), lambda b, i: (b, i, 0)),
            _layer_spec(w_in, layer, 2),
            _layer_spec(gq, layer, 2),
            _layer_spec(gkv, layer, 2),
            _layer_spec(w_uq, layer, 2),
            _layer_spec(w_ukv, layer, 2),
            pl.BlockSpec((ts, LANES), lambda b, i: (i, 0)),
        ],
        out_specs=(
            pl.BlockSpec((None, ts, QA_COLS), lambda b, i: (b, i, 0)),
            pl.BlockSpec((None, ts, KA_COLS), lambda b, i: (b, i, 0)),
            pl.BlockSpec((None, N_KV_A, HEAD_DIM, ts), lambda b, i: (b, 0, 0, i)),
            pl.BlockSpec((None, N_HEADS_B, ts, QK_B), lambda b, i: (b, 0, i, 0)),
            pl.BlockSpec((None, N_HEADS_B, ts, QK_B), lambda b, i: (b, 0, i, 0)),
            pl.BlockSpec((None, N_HEADS_B, D_V, ts), lambda b, i: (b, 0, 0, i)),
        ),
        out_shape=out_shape,
        compiler_params=_cparams(("parallel", "parallel")),
        name="in_proj",
    )(x, w_in, gq, gkv, w_uq, w_ukv, cs)


def _swa_kernel(sink_ref, q_ref, k_ref, vt_ref, bias_ref, o_ref, *, layer, tq):
    S = k_ref.shape[0]
    nb = S // BLOCK
    qi = pl.program_id(1)
    blocks = [(j, g) for j in range(tq // BLOCK) for g in range(N_KV_A)]

    def window(j):
        n = qi * (tq // BLOCK) + j
        case = jnp.where(n == 0, 0, jnp.where(n == nb - 1, 2, 1))
        rows = [pl.multiple_of(jnp.maximum(n - 1, 0) * BLOCK, BLOCK), pl.multiple_of(n * BLOCK, BLOCK),
                pl.multiple_of(jnp.minimum(n + 1, nb - 1) * BLOCK, BLOCK)]
        return case, rows

    def scores(j, g):
        case, rows = window(j)
        cols = slice(g * HEAD_DIM, (g + 1) * HEAD_DIM)
        kw = jnp.concatenate([k_ref[pl.ds(r, BLOCK), cols] for r in rows], axis=0)
        qs = jnp.concatenate([q_ref[j * BLOCK:(j + 1) * BLOCK, hd * HEAD_DIM:(hd + 1) * HEAD_DIM]
                              for hd in range(g * GROUP_A, (g + 1) * GROUP_A)], axis=0)
        s = lax.dot_general(kw, qs, (((1,), (1,)), ((), ())), preferred_element_type=F32)
        return s + bias_ref[case, g]

    s_next = scores(*blocks[0])
    for idx, (j, g) in enumerate(blocks):
        s = s_next
        if idx + 1 < len(blocks):
            s_next = scores(*blocks[idx + 1])
        _, rows = window(j)
        heads = range(g * GROUP_A, (g + 1) * GROUP_A)
        vwt = jnp.concatenate([vt_ref[g, :, pl.ds(r, BLOCK)] for r in rows], axis=1)
        sink = jnp.concatenate([jnp.full((1, BLOCK), sink_ref[layer, hd] * LOG2E, F32) for hd in heads], axis=1)
        m = jnp.maximum(jnp.max(s, axis=0, keepdims=True), sink)
        p = jnp.exp2(s - m)
        l = jnp.sum(p, axis=0, keepdims=True) + jnp.exp2(sink - m)
        o = jnp.dot(vwt, p.astype(BF16), preferred_element_type=F32) * (1.0 / l)
        for t, hd in enumerate(heads):
            o_ref[j * BLOCK:(j + 1) * BLOCK, hd * HEAD_DIM:(hd + 1) * HEAD_DIM] = (
                o[:, t * BLOCK:(t + 1) * BLOCK].T.astype(o_ref.dtype))


def _swa(sink, qa, ka, vat, bias, *, layer, tq):
    B, S, _ = qa.shape
    grid = (B, S // tq)
    return pl.pallas_call(
        functools.partial(_swa_kernel, layer=layer, tq=tq),
        grid=grid,
        in_specs=[
            pl.BlockSpec(memory_space=pltpu.SMEM),
            pl.BlockSpec((None, tq, QA_COLS), lambda b, i: (b, i, 0)),
            pl.BlockSpec((None, S, KA_COLS), lambda b, i: (b, 0, 0)),
            pl.BlockSpec((None, N_KV_A, HEAD_DIM, S), lambda b, i: (b, 0, 0, 0)),
            pl.BlockSpec(bias.shape, lambda b, i: (0, 0, 0, 0)),
        ],
        out_specs=pl.BlockSpec((None, tq, QA_COLS), lambda b, i: (b, i, 0)),
        out_shape=jax.ShapeDtypeStruct((B, S, QA_COLS), BF16),
        compiler_params=_cparams(("parallel", "parallel")),
        name="swa",
    )(sink, qa, ka, vat, bias)


def _mla_kernel(q_ref, k_ref, vt_ref, o_ref, *, ck):
    S = k_ref.shape[0]
    nck = S // ck
    q = q_ref[...]

    def scores(c):
        return lax.dot_general(k_ref[c * ck:(c + 1) * ck, :], q, (((1,), (1,)), ((), ())),
                               preferred_element_type=F32)

    s = scores(0)
    m = l = acc = None
    for c in range(nck):
        s_next = scores(c + 1) if c + 1 < nck else None
        cmax = jnp.max(s, axis=0, keepdims=True)
        m_new = cmax if c == 0 else jnp.maximum(m, cmax)
        p = jnp.exp2(s - m_new)
        psum = jnp.sum(p, axis=0, keepdims=True)
        pv = jnp.dot(vt_ref[:, c * ck:(c + 1) * ck], p.astype(BF16), preferred_element_type=F32)
        if c == 0:
            l, acc = psum, pv
        else:
            rescale = jnp.exp2(m - m_new)
            l = rescale * l + psum
            acc = rescale * acc + pv
        m, s = m_new, s_next
    o = acc * (1.0 / l)
    o_ref[...] = o.T.astype(o_ref.dtype)


def _mla(qb, kb, vt, *, tq, ck):
    B, H, S, _ = qb.shape
    grid = (B, H, S // tq)
    return pl.pallas_call(
        functools.partial(_mla_kernel, ck=ck),
        grid=grid,
        in_specs=[
            pl.BlockSpec((None, None, tq, QK_B), lambda b, h, i: (b, h, i, 0)),
            pl.BlockSpec((None, None, S, QK_B), lambda b, h, i: (b, h, 0, 0)),
            pl.BlockSpec((None, None, D_V, S), lambda b, h, i: (b, h, 0, 0)),
        ],
        out_specs=pl.BlockSpec((None, tq, D_V), lambda b, h, i: (b, i, h)),
        out_shape=jax.ShapeDtypeStruct((B, S, H * D_V), BF16),
        compiler_params=_cparams(("parallel", "parallel", "parallel")),
        name="mla",
    )(qb, kb, vt)


def _layer_norm(y, g, b):
    mu = jnp.mean(y, axis=-1, keepdims=True)
    d = y - mu
    var = jnp.mean(d * d, axis=-1, keepdims=True)
    return d * lax.rsqrt(var + LN_EPS) * g + b


def _out_ln_kernel(oa_ref, ob_ref, x_ref, wo_ref, g_ref, b_ref, o_ref, *, alpha, n_sub):
    ts = x_ref.shape[0]
    rs = ts // n_sub
    for r in range(n_sub):
        rows = slice(r * rs, (r + 1) * rs)
        o_cat = jnp.concatenate([oa_ref[rows, :], ob_ref[rows, :]], axis=-1)
        attn = jnp.dot(o_cat, wo_ref[...], preferred_element_type=F32)
        y = alpha * x_ref[rows, :] + attn
        o_ref[rows, :] = _layer_norm(y, g_ref[...], b_ref[...])


def _out_ln(oa, ob, x, w_o, g, b, *, layer, alpha, ts, n_sub):
    B, S, D = x.shape
    na, nb = oa.shape[-1], ob.shape[-1]
    grid = (B, S // ts)
    return pl.pallas_call(
        functools.partial(_out_ln_kernel, alpha=alpha, n_sub=n_sub),
        grid=grid,
        in_specs=[
            pl.BlockSpec((None, ts, na), lambda b_, i: (b_, i, 0)),
            pl.BlockSpec((None, ts, nb), lambda b_, i: (b_, i, 0)),
            pl.BlockSpec((None, ts, D), lambda b_, i: (b_, i, 0)),
            _layer_spec(w_o, layer, 2),
            _layer_spec(g, layer, 2),
            _layer_spec(b, layer, 2),
        ],
        out_specs=pl.BlockSpec((None, ts, D), lambda b_, i: (b_, i, 0)),
        out_shape=jax.ShapeDtypeStruct((B, S, D), F32),
        compiler_params=_cparams(("parallel", "parallel")),
        name="out_ln",
    )(oa, ob, x, w_o, g, b)


HALO = 8


def _ffn_kernel(xp_ref, x_ref, xn_ref, wg_ref, wv_ref, cwg_ref, cwv_ref, cbg_ref, cbv_ref, wd_ref,
                g_ref, b_ref, o_ref, xb_scr, act_scr, *, alpha):
    i = pl.program_id(1)
    c = pl.program_id(2)
    nt = pl.num_programs(1)
    nck = pl.num_programs(2) - 1
    tm = x_ref.shape[0]

    def up_phase():
        xb = xb_scr[...]
        row = lax.broadcasted_iota(jnp.int32, (tm, 1), 0)
        has_prev = (i > 0).astype(F32)
        has_next = (i < nt - 1).astype(F32)

        def conv_branch(w_ref, cw_ref, cb_ref):
            u_ext = jnp.dot(xb, w_ref[...], preferred_element_type=F32)
            u = u_ext[0:tm]
            u_prev = u_ext[tm + HALO - 1:tm + HALO] * has_prev
            u_next = u_ext[tm + HALO:tm + HALO + 1] * has_next
            u_dn = jnp.where(row == 0, u_prev, pltpu.roll(u, 1, 0))
            u_up = jnp.where(row == tm - 1, u_next, pltpu.roll(u, tm - 1, 0))
            return u_dn * cw_ref[0:1, :] + u * cw_ref[1:2, :] + u_up * cw_ref[2:3, :] + cb_ref[...]

        gate = conv_branch(wg_ref, cwg_ref, cbg_ref)
        val = conv_branch(wv_ref, cwv_ref, cbv_ref)
        act_scr[c % 2] = (gate * (1.0 / (1.0 + jnp.exp(-gate))) * val).astype(BF16)

    def down_part():
        return jnp.dot(act_scr[(c + 1) % 2], wd_ref[...], preferred_element_type=F32)

    @pl.when(c == 0)
    def _():
        xb_scr[0:tm, :] = x_ref[...].astype(BF16)
        xb_scr[tm:tm + 2 * HALO, :] = jnp.concatenate([xp_ref[...], xn_ref[...]], axis=0).astype(BF16)
        o_ref[...] = jnp.zeros_like(o_ref)
        up_phase()

    @pl.when(jnp.logical_and(c > 0, c < nck))
    def _():
        up_phase()
        o_ref[...] += down_part()

    @pl.when(c == nck)
    def _():
        y = alpha * x_ref[...] + (o_ref[...] + down_part())
        o_ref[...] = _layer_norm(y, g_ref[...], b_ref[...])


def _ffn(x, w_up, conv_w, conv_b, w_down, g, b, *, layer, alpha, tm, ck):
    B, S, D = x.shape
    F = w_down.shape[1]
    nck = F // ck
    nh = tm // HALO
    grid = (B, S // tm, nck + 1)
    last_halo = S // HALO - 1
    up_idx = lambda c: jnp.minimum(c, nck - 1)
    dn_idx = lambda c: jnp.maximum(c - 1, 0)
    return pl.pallas_call(
        functools.partial(_ffn_kernel, alpha=alpha),
        grid=grid,
        in_specs=[
            pl.BlockSpec((None, HALO, D), lambda b_, i, c: (b_, jnp.maximum(i * nh - 1, 0), 0)),
            pl.BlockSpec((None, tm, D), lambda b_, i, c: (b_, i, 0)),
            pl.BlockSpec((None, HALO, D), lambda b_, i, c: (b_, jnp.minimum((i + 1) * nh, last_halo), 0)),
            pl.BlockSpec((None, D, ck), lambda b_, i, c: (layer, 0, up_idx(c))),
            pl.BlockSpec((None, D, ck), lambda b_, i, c: (layer, 0, nck + up_idx(c))),
            pl.BlockSpec((None, 3, ck), lambda b_, i, c: (layer, 0, up_idx(c))),
            pl.BlockSpec((None, 3, ck), lambda b_, i, c: (layer, 0, nck + up_idx(c))),
            pl.BlockSpec((None, 1, ck), lambda b_, i, c: (layer, 0, up_idx(c))),
            pl.BlockSpec((None, 1, ck), lambda b_, i, c: (layer, 0, nck + up_idx(c))),
            pl.BlockSpec((None, ck, D), lambda b_, i, c: (layer, dn_idx(c), 0)),
            _layer_spec(g, layer, 3),
            _layer_spec(b, layer, 3),
        ],
        out_specs=pl.BlockSpec((None, tm, D), lambda b_, i, c: (b_, i, 0)),
        out_shape=jax.ShapeDtypeStruct((B, S, D), F32),
        scratch_shapes=[pltpu.VMEM((tm + 2 * HALO, D), BF16), pltpu.VMEM((2, tm, ck), BF16)],
        compiler_params=_cparams(("parallel", "parallel", "arbitrary")),
        name="ffn",
    )(x, x, x, w_up, w_up, conv_w, conv_w, conv_b, conv_b, w_down, g, b)


def _t5_bucket(rel):
    half = N_BUCKETS // 2
    max_exact = half // 2
    ret = (rel > 0).astype(jnp.int32) * half
    n = jnp.abs(rel)
    large = max_exact + (jnp.log(jnp.maximum(n, 1).astype(jnp.float32) / max_exact)
                         / math.log(MAX_DISTANCE / max_exact) * (half - max_exact)).astype(jnp.int32)
    large = jnp.minimum(large, half - 1)
    return ret + jnp.where(n < max_exact, n, large)


def _window_bias(rel_bias):
    qi = jnp.arange(BLOCK, dtype=jnp.int32)
    ki = jnp.arange(3 * BLOCK, dtype=jnp.int32)
    rel = ki[None, :] - BLOCK - qi[:, None]
    bucket = _t5_bucket(rel)
    table = rel_bias.astype(F32).T
    hit = bucket[None, :, :, None] == jnp.arange(N_BUCKETS, dtype=jnp.int32)
    bias = jnp.sum(jnp.where(hit, table[:, None, None, :], 0.0), axis=-1)
    base = jnp.where((jnp.abs(rel) <= WINDOW)[None], bias, NEG_INF)
    first = jnp.where((ki >= BLOCK)[None, None, :], base, NEG_INF)
    last = jnp.where((ki < 2 * BLOCK)[None, None, :], base, NEG_INF)
    tables = jnp.stack([first, base, last]) * LOG2E
    tables = tables.reshape(3, N_KV_A, GROUP_A, BLOCK, 3 * BLOCK).transpose(0, 1, 4, 2, 3)
    return tables.reshape(3, N_KV_A, 3 * BLOCK, GROUP_A * BLOCK)


def _rope_table(S):
    inv = 1.0 / (ROPE_THETA ** (jnp.arange(0, D_ROPE, 2, dtype=F32) / D_ROPE))
    ang = jnp.arange(S, dtype=F32)[:, None] * inv[None, :]
    cos, sin = jnp.cos(ang), jnp.sin(ang)
    return jnp.concatenate([cos, cos, -sin, sin], axis=-1)


def _swap_halves(w):
    return jnp.concatenate([w[..., D_ROPE // 2:], w[..., :D_ROPE // 2]], axis=-1)


def _tiles(S):
    mla_elems = 4 * 1024 * 1024
    tq_mla = max(256, min(S, mla_elems // S))
    ck_mla = S // 2
    return dict(ts=512, tq_swa=512, tq_mla=tq_mla, ck_mla=ck_mla, ts_out=512, sub_out=2,
                tm_ffn=min(1024, S), ck_ffn=512)


def kernel(x_prompt, x_sample, rel_bias, w_in, sink, q_norm_g, w_uq, kv_norm_g, w_ukv, w_o,
           ln1_g, ln1_b, w_up, conv_w, conv_b, w_down, ln2_g, ln2_b):
    depth = w_in.shape[0]
    alpha = (2 * depth) ** 0.25
    in_cols = w_in.shape[-1]
    w_in_x = jnp.concatenate([w_in, _swap_halves(w_in[..., in_cols - D_ROPE:])], axis=-1).astype(BF16)
    wq = w_uq.reshape(depth, Q_LORA, N_HEADS_B, D_NOPE + D_ROPE)
    w_uq_x = jnp.concatenate([wq, _swap_halves(wq[..., D_NOPE:])], axis=-1)
    w_uq_x = w_uq_x.reshape(depth, Q_LORA, N_HEADS_B * QK_B).astype(BF16)
    w_ukv_b = w_ukv.astype(BF16)
    w_o_b = w_o.astype(BF16)
    w_up_b = w_up.astype(BF16)
    w_down_b = w_down.astype(BF16)
    row = lambda p: p[:, None, :]
    gq, gkv = row(q_norm_g), row(kv_norm_g)
    g1, b1, g2, b2 = row(ln1_g), row(ln1_b), row(ln2_g), row(ln2_b)
    cb = row(conv_b)
    bias = _window_bias(rel_bias)

    def trunk(x):
        S = x.shape[1]
        t = _tiles(S)
        cs = _rope_table(S)
        for l in range(depth):
            qa, ka, vat, qb, kb, vt = _in_proj(x, w_in_x, gq, gkv, w_uq_x, w_ukv_b, cs, layer=l, ts=t["ts"])
            oa = _swa(sink, qa, ka, vat, bias, layer=l, tq=t["tq_swa"])
            ob = _mla(qb, kb, vt, tq=t["tq_mla"], ck=t["ck_mla"])
            x = _out_ln(oa, ob, x, w_o_b, g1, b1, layer=l, alpha=alpha, ts=t["ts_out"], n_sub=t["sub_out"])
            x = _ffn(x, w_up_b, conv_w, cb, w_down_b, g2, b2, layer=l, alpha=alpha,
                     tm=t["tm_ffn"], ck=t["ck_ffn"])
        return x

    return trunk(x_prompt), trunk(x_sample)
```

```python
import functools
import math

import jax
import jax.numpy as jnp
from jax import lax
from jax.experimental import pallas as pl
from jax.experimental.pallas import tpu as pltpu

HEAD_DIM = 128
N_HEADS_A = 8
N_KV_A = 2
GROUP_A = N_HEADS_A // N_KV_A
WINDOW = 128
BLOCK = 128
N_BUCKETS = 32
MAX_DISTANCE = 128
N_HEADS_B = 8
Q_LORA = 512
KV_LORA = 256
D_NOPE = 128
D_ROPE = 64
D_V = 128
ROPE_THETA = 10000.0
LN_EPS = 1e-5
RMS_EPS = 1e-6
NEG_INF = -1e30
LOG2E = math.log2(math.e)

QA_COLS = N_HEADS_A * HEAD_DIM
KA_COLS = N_KV_A * HEAD_DIM
QK_B = D_NOPE + 2 * D_ROPE
KV_B = D_NOPE + D_V

LANES = 128
V7X_VMEM_BYTES = 64 * 1024 * 1024
VMEM_LIMIT = V7X_VMEM_BYTES - 3 * 1024 * 1024

BF16 = jnp.bfloat16
F32 = jnp.float32


def _cparams(semantics):
    return pltpu.CompilerParams(dimension_semantics=semantics, vmem_limit_bytes=VMEM_LIMIT)


def _layer_spec(arr, layer, n_grid):
    zeros = (0,) * (arr.ndim - 1)
    index_map = {2: lambda a, b: (layer,) + zeros, 3: lambda a, b, c: (layer,) + zeros}[n_grid]
    return pl.BlockSpec((None,) + arr.shape[1:], index_map, pipeline_mode=pl.Buffered(1))


def _in_proj_kernel(x_ref, w_in_ref, gq_ref, gkv_ref, w_uq_ref, w_ukv_ref, cs_ref,
                    qa_ref, ka_ref, vat_ref, qb_ref, kb_ref, vt_ref, *, scale_a, scale_b):
    xb = x_ref[...].astype(BF16)
    h = jnp.dot(xb, w_in_ref[...], preferred_element_type=F32)
    c0 = 0
    qa_ref[...] = (h[:, c0:c0 + QA_COLS] * scale_a).astype(BF16)
    c0 += QA_COLS
    ka_ref[...] = h[:, c0:c0 + KA_COLS].astype(BF16)
    c0 += KA_COLS
    for g in range(N_KV_A):
        vat_ref[g] = h[:, c0 + g * HEAD_DIM:c0 + (g + 1) * HEAD_DIM].T.astype(BF16)
    c0 += KA_COLS
    cq = h[:, c0:c0 + Q_LORA]
    c0 += Q_LORA
    ckv = h[:, c0:c0 + KV_LORA]
    c0 += KV_LORA
    kr_pair = h[:, c0:c0 + 2 * D_ROPE]

    cqn = cq * lax.rsqrt(jnp.mean(cq * cq, axis=-1, keepdims=True) + RMS_EPS) * gq_ref[...]
    ckvn = ckv * lax.rsqrt(jnp.mean(ckv * ckv, axis=-1, keepdims=True) + RMS_EPS) * gkv_ref[...]
    qb = jnp.dot(cqn.astype(BF16), w_uq_ref[...], preferred_element_type=F32)
    kvb = jnp.dot(ckvn.astype(BF16), w_ukv_ref[...], preferred_element_type=F32)

    cs = cs_ref[...]
    lane = lax.broadcasted_iota(jnp.int32, cs.shape, 1)
    kt = kr_pair * cs
    kr = kt + pltpu.roll(kt, D_ROPE, 1)
    kr = jnp.where(lane < D_ROPE, kr, 0.0).astype(BF16)
    for hd in range(N_HEADS_B):
        q0 = hd * QK_B
        k0 = hd * KV_B
        qt = qb[:, q0 + D_NOPE:q0 + QK_B] * cs
        qr = qt + pltpu.roll(qt, D_ROPE, 1)
        qb_ref[hd, :, 0:D_NOPE] = (qb[:, q0:q0 + D_NOPE] * scale_b).astype(BF16)
        qb_ref[hd, :, D_NOPE:QK_B] = (qr * scale_b).astype(BF16)
        kb_ref[hd, :, 0:D_NOPE] = kvb[:, k0:k0 + D_NOPE].astype(BF16)
        kb_ref[hd, :, D_NOPE:QK_B] = kr
        vt_ref[hd] = kvb[:, k0 + D_NOPE:k0 + KV_B].T.astype(BF16)


def _in_proj(x, w_in, gq, gkv, w_uq, w_ukv, cs, *, layer, ts):
    B, S, D = x.shape
    scale_a = HEAD_DIM ** -0.5 * LOG2E
    scale_b = (D_NOPE + D_ROPE) ** -0.5 * LOG2E
    grid = (B, S // ts)
    out_shape = (
        jax.ShapeDtypeStruct((B, S, QA_COLS), BF16),
        jax.ShapeDtypeStruct((B, S, KA_COLS), BF16),
        jax.ShapeDtypeStruct((B, N_KV_A, HEAD_DIM, S), BF16),
        jax.ShapeDtypeStruct((B, N_HEADS_B, S, QK_B), BF16),
        jax.ShapeDtypeStruct((B, N_HEADS_B, S, QK_B), BF16),
        jax.ShapeDtypeStruct((B, N_HEADS_B, D_V, S), BF16),
    )
    return pl.pallas_call(
        functools.partial(_in_proj_kernel, scale_a=scale_a, scale_b=scale_b),
        grid=grid,
        in_specs=[
            pl.BlockSpec((None, ts, D), lambda b, i: (b, i, 0)),
            _layer_spec(w_in, layer, 2),
            _layer_spec(gq, layer, 2),
            _layer_spec(gkv, layer, 2),
            _layer_spec(w_uq, layer, 2),
            _layer_spec(w_ukv, layer, 2),
            pl.BlockSpec((ts, LANES), lambda b, i: (i, 0)),
        ],
        out_specs=(
            pl.BlockSpec((None, ts, QA_COLS), lambda b, i: (b, i, 0)),
            pl.BlockSpec((None, ts, KA_COLS), lambda b, i: (b, i, 0)),
            pl.BlockSpec((None, N_KV_A, HEAD_DIM, ts), lambda b, i: (b, 0, 0, i)),
            pl.BlockSpec((None, N_HEADS_B, ts, QK_B), lambda b, i: (b, 0, i, 0)),
            pl.BlockSpec((None, N_HEADS_B, ts, QK_B), lambda b, i: (b, 0, i, 0)),
            pl.BlockSpec((None, N_HEADS_B, D_V, ts), lambda b, i: (b, 0, 0, i)),
        ),
        out_shape=out_shape,
        compiler_params=_cparams(("parallel", "parallel")),
        name="in_proj",
    )(x, w_in, gq, gkv, w_uq, w_ukv, cs)


def _swa_kernel(sink_ref, q_ref, k_ref, vt_ref, bias_ref, o_ref, *, layer, tq):
    S = k_ref.shape[0]
    nb = S // BLOCK
    qi = pl.program_id(1)
    blocks = [(j, g) for j in range(tq // BLOCK) for g in range(N_KV_A)]

    def window(j):
        n = qi * (tq // BLOCK) + j
        case = jnp.where(n == 0, 0, jnp.where(n == nb - 1, 2, 1))
        rows = [pl.multiple_of(jnp.maximum(n - 1, 0) * BLOCK, BLOCK), pl.multiple_of(n * BLOCK, BLOCK),
                pl.multiple_of(jnp.minimum(n + 1, nb - 1) * BLOCK, BLOCK)]
        return case, rows

    def scores(j, g):
        case, rows = window(j)
        cols = slice(g * HEAD_DIM, (g + 1) * HEAD_DIM)
        kw = jnp.concatenate([k_ref[pl.ds(r, BLOCK), cols] for r in rows], axis=0)
        qs = jnp.concatenate([q_ref[j * BLOCK:(j + 1) * BLOCK, hd * HEAD_DIM:(hd + 1) * HEAD_DIM]
                              for hd in range(g * GROUP_A, (g + 1) * GROUP_A)], axis=0)
        s = lax.dot_general(kw, qs, (((1,), (1,)), ((), ())), preferred_element_type=F32)
        return s + bias_ref[case, g]

    s_next = scores(*blocks[0])
    for idx, (j, g) in enumerate(blocks):
        s = s_next
        if idx + 1 < len(blocks):
            s_next = scores(*blocks[idx + 1])
        _, rows = window(j)
        heads = range(g * GROUP_A, (g + 1) * GROUP_A)
        vwt = jnp.concatenate([vt_ref[g, :, pl.ds(r, BLOCK)] for r in rows], axis=1)
        sink = jnp.concatenate([jnp.full((1, BLOCK), sink_ref[layer, hd] * LOG2E, F32) for hd in heads], axis=1)
        m = jnp.maximum(jnp.max(s, axis=0, keepdims=True), sink)
        p = jnp.exp2(s - m)
        l = jnp.sum(p, axis=0, keepdims=True) + jnp.exp2(sink - m)
        o = jnp.dot(vwt, p.astype(BF16), preferred_element_type=F32) * (1.0 / l)
        for t, hd in enumerate(heads):
            o_ref[j * BLOCK:(j + 1) * BLOCK, hd * HEAD_DIM:(hd + 1) * HEAD_DIM] = (
                o[:, t * BLOCK:(t + 1) * BLOCK].T.astype(o_ref.dtype))


def _swa(sink, qa, ka, vat, bias, *, layer, tq):
    B, S, _ = qa.shape
    grid = (B, S // tq)
    return pl.pallas_call(
        functools.partial(_swa_kernel, layer=layer, tq=tq),
        grid=grid,
        in_specs=[
            pl.BlockSpec(memory_space=pltpu.SMEM),
            pl.BlockSpec((None, tq, QA_COLS), lambda b, i: (b, i, 0)),
            pl.BlockSpec((None, S, KA_COLS), lambda b, i: (b, 0, 0)),
            pl.BlockSpec((None, N_KV_A, HEAD_DIM, S), lambda b, i: (b, 0, 0, 0)),
            pl.BlockSpec(bias.shape, lambda b, i: (0, 0, 0, 0)),
        ],
        out_specs=pl.BlockSpec((None, tq, QA_COLS), lambda b, i: (b, i, 0)),
        out_shape=jax.ShapeDtypeStruct((B, S, QA_COLS), BF16),
        compiler_params=_cparams(("parallel", "parallel")),
        name="swa",
    )(sink, qa, ka, vat, bias)


def _mla_kernel(q_ref, k_ref, vt_ref, o_ref, *, ck):
    S = k_ref.shape[0]
    nck = S // ck
    q = q_ref[...]

    def scores(c):
        return lax.dot_general(k_ref[c * ck:(c + 1) * ck, :], q, (((1,), (1,)), ((), ())),
                               preferred_element_type=F32)

    s = scores(0)
    m = l = acc = None
    for c in range(nck):
        s_next = scores(c + 1) if c + 1 < nck else None
        cmax = jnp.max(s, axis=0, keepdims=True)
        m_new = cmax if c == 0 else jnp.maximum(m, cmax)
        p = jnp.exp2(s - m_new)
        psum = jnp.sum(p, axis=0, keepdims=True)
        pv = jnp.dot(vt_ref[:, c * ck:(c + 1) * ck], p.astype(BF16), preferred_element_type=F32)
        if c == 0:
            l, acc = psum, pv
        else:
            rescale = jnp.exp2(m - m_new)
            l = rescale * l + psum
            acc = rescale * acc + pv
        m, s = m_new, s_next
    o = acc * (1.0 / l)
    o_ref[...] = o.T.astype(o_ref.dtype)


def _mla(qb, kb, vt, *, tq, ck):
    B, H, S, _ = qb.shape
    grid = (B, H, S // tq)
    return pl.pallas_call(
        functools.partial(_mla_kernel, ck=ck),
        grid=grid,
        in_specs=[
            pl.BlockSpec((None, None, tq, QK_B), lambda b, h, i: (b, h, i, 0)),
            pl.BlockSpec((None, None, S, QK_B), lambda b, h, i: (b, h, 0, 0)),
            pl.BlockSpec((None, None, D_V, S), lambda b, h, i: (b, h, 0, 0)),
        ],
        out_specs=pl.BlockSpec((None, tq, D_V), lambda b, h, i: (b, i, h)),
        out_shape=jax.ShapeDtypeStruct((B, S, H * D_V), BF16),
        compiler_params=_cparams(("parallel", "parallel", "parallel")),
        name="mla",
    )(qb, kb, vt)


def _layer_norm(y, g, b):
    mu = jnp.mean(y, axis=-1, keepdims=True)
    d = y - mu
    var = jnp.mean(d * d, axis=-1, keepdims=True)
    return d * lax.rsqrt(var + LN_EPS) * g + b


def _out_ln_kernel(oa_ref, ob_ref, x_ref, wo_ref, g_ref, b_ref, o_ref, *, alpha, n_sub):
    ts = x_ref.shape[0]
    rs = ts // n_sub
    for r in range(n_sub):
        rows = slice(r * rs, (r + 1) * rs)
        o_cat = jnp.concatenate([oa_ref[rows, :], ob_ref[rows, :]], axis=-1)
        attn = jnp.dot(o_cat, wo_ref[...], preferred_element_type=F32)
        y = alpha * x_ref[rows, :] + attn
        o_ref[rows, :] = _layer_norm(y, g_ref[...], b_ref[...])


def _out_ln(oa, ob, x, w_o, g, b, *, layer, alpha, ts, n_sub):
    B, S, D = x.shape
    na, nb = oa.shape[-1], ob.shape[-1]
    grid = (B, S // ts)
    return pl.pallas_call(
        functools.partial(_out_ln_kernel, alpha=alpha, n_sub=n_sub),
        grid=grid,
        in_specs=[
            pl.BlockSpec((None, ts, na), lambda b_, i: (b_, i, 0)),
            pl.BlockSpec((None, ts, nb), lambda b_, i: (b_, i, 0)),
            pl.BlockSpec((None, ts, D), lambda b_, i: (b_, i, 0)),
            _layer_spec(w_o, layer, 2),
            _layer_spec(g, layer, 2),
            _layer_spec(b, layer, 2),
        ],
        out_specs=pl.BlockSpec((None, ts, D), lambda b_, i: (b_, i, 0)),
        out_shape=jax.ShapeDtypeStruct((B, S, D), F32),
        compiler_params=_cparams(("parallel", "parallel")),
        name="out_ln",
    )(oa, ob, x, w_o, g, b)


HALO = 8


def _dot_row_halves(a, w):
    half = a.shape[0] // 2 // 16 * 16
    return jnp.concatenate([jnp.dot(a[:half], w, preferred_element_type=F32),
                            jnp.dot(a[half:], w, preferred_element_type=F32)], axis=0)


def _ffn_kernel(xp_ref, x_ref, xn_ref, wg_ref, wv_ref, cwg_ref, cwv_ref, cbg_ref, cbv_ref, wd_ref,
                g_ref, b_ref, o_ref, xb_scr, act_scr, *, alpha):
    i = pl.program_id(1)
    c = pl.program_id(2)
    nt = pl.num_programs(1)
    nck = pl.num_programs(2) - 1
    tm = x_ref.shape[0]

    def up_phase():
        xb = xb_scr[...]
        row = lax.broadcasted_iota(jnp.int32, (tm, 1), 0)
        has_prev = (i > 0).astype(F32)
        has_next = (i < nt - 1).astype(F32)

        def conv_branch(w_ref, cw_ref, cb_ref):
            u_ext = _dot_row_halves(xb, w_ref[...])
            u = u_ext[0:tm]
            u_prev = u_ext[tm + HALO - 1:tm + HALO] * has_prev
            u_next = u_ext[tm + HALO:tm + HALO + 1] * has_next
            u_dn = jnp.where(row == 0, u_prev, pltpu.roll(u, 1, 0))
            u_up = jnp.where(row == tm - 1, u_next, pltpu.roll(u, tm - 1, 0))
            return u_dn * cw_ref[0:1, :] + u * cw_ref[1:2, :] + u_up * cw_ref[2:3, :] + cb_ref[...]

        gate = conv_branch(wg_ref, cwg_ref, cbg_ref)
        val = conv_branch(wv_ref, cwv_ref, cbv_ref)
        act_scr[c % 2] = (gate * (1.0 / (1.0 + jnp.exp(-gate))) * val).astype(BF16)

    def down_part():
        return _dot_row_halves(act_scr[(c + 1) % 2], wd_ref[...])

    @pl.when(c == 0)
    def _():
        xb_scr[0:tm, :] = x_ref[...].astype(BF16)
        xb_scr[tm:tm + 2 * HALO, :] = jnp.concatenate([xp_ref[...], xn_ref[...]], axis=0).astype(BF16)
        o_ref[...] = jnp.zeros_like(o_ref)
        up_phase()

    @pl.when(jnp.logical_and(c > 0, c < nck))
    def _():
        up_phase()
        o_ref[...] += down_part()

    @pl.when(c == nck)
    def _():
        y = alpha * x_ref[...] + (o_ref[...] + down_part())
        o_ref[...] = _layer_norm(y, g_ref[...], b_ref[...])


def _ffn(x, w_up, conv_w, conv_b, w_down, g, b, *, layer, alpha, tm, ck):
    B, S, D = x.shape
    F = w_down.shape[1]
    nck = F // ck
    nh = tm // HALO
    grid = (B, S // tm, nck + 1)
    last_halo = S // HALO - 1
    up_idx = lambda c: jnp.minimum(c, nck - 1)
    dn_idx = lambda c: jnp.maximum(c - 1, 0)
    return pl.pallas_call(
        functools.partial(_ffn_kernel, alpha=alpha),
        grid=grid,
        in_specs=[
            pl.BlockSpec((None, HALO, D), lambda b_, i, c: (b_, jnp.maximum(i * nh - 1, 0), 0)),
            pl.BlockSpec((None, tm, D), lambda b_, i, c: (b_, i, 0)),
            pl.BlockSpec((None, HALO, D), lambda b_, i, c: (b_, jnp.minimum((i + 1) * nh, last_halo), 0)),
            pl.BlockSpec((None, D, ck), lambda b_, i, c: (layer, 0, up_idx(c))),
            pl.BlockSpec((None, D, ck), lambda b_, i, c: (layer, 0, nck + up_idx(c))),
            pl.BlockSpec((None, 3, ck), lambda b_, i, c: (layer, 0, up_idx(c))),
            pl.BlockSpec((None, 3, ck), lambda b_, i, c: (layer, 0, nck + up_idx(c))),
            pl.BlockSpec((None, 1, ck), lambda b_, i, c: (layer, 0, up_idx(c))),
            pl.BlockSpec((None, 1, ck), lambda b_, i, c: (layer, 0, nck + up_idx(c))),
            pl.BlockSpec((None, ck, D), lambda b_, i, c: (layer, dn_idx(c), 0)),
            _layer_spec(g, layer, 3),
            _layer_spec(b, layer, 3),
        ],
        out_specs=pl.BlockSpec((None, tm, D), lambda b_, i, c: (b_, i, 0)),
        out_shape=jax.ShapeDtypeStruct((B, S, D), F32),
        scratch_shapes=[pltpu.VMEM((tm + 2 * HALO, D), BF16), pltpu.VMEM((2, tm, ck), BF16)],
        compiler_params=_cparams(("parallel", "parallel", "arbitrary")),
        name="ffn",
    )(x, x, x, w_up, w_up, conv_w, conv_w, conv_b, conv_b, w_down, g, b)


def _t5_bucket(rel):
    half = N_BUCKETS // 2
    max_exact = half // 2
    ret = (rel > 0).astype(jnp.int32) * half
    n = jnp.abs(rel)
    large = max_exact + (jnp.log(jnp.maximum(n, 1).astype(jnp.float32) / max_exact)
                         / math.log(MAX_DISTANCE / max_exact) * (half - max_exact)).astype(jnp.int32)
    large = jnp.minimum(large, half - 1)
    return ret + jnp.where(n < max_exact, n, large)


def _window_bias(rel_bias):
    qi = jnp.arange(BLOCK, dtype=jnp.int32)
    ki = jnp.arange(3 * BLOCK, dtype=jnp.int32)
    rel = ki[None, :] - BLOCK - qi[:, None]
    bucket = _t5_bucket(rel)
    table = rel_bias.astype(F32).T
    hit = bucket[None, :, :, None] == jnp.arange(N_BUCKETS, dtype=jnp.int32)
    bias = jnp.sum(jnp.where(hit, table[:, None, None, :], 0.0), axis=-1)
    base = jnp.where((jnp.abs(rel) <= WINDOW)[None], bias, NEG_INF)
    first = jnp.where((ki >= BLOCK)[None, None, :], base, NEG_INF)
    last = jnp.where((ki < 2 * BLOCK)[None, None, :], base, NEG_INF)
    tables = jnp.stack([first, base, last]) * LOG2E
    tables = tables.reshape(3, N_KV_A, GROUP_A, BLOCK, 3 * BLOCK).transpose(0, 1, 4, 2, 3)
    return tables.reshape(3, N_KV_A, 3 * BLOCK, GROUP_A * BLOCK)


def _rope_table(S):
    inv = 1.0 / (ROPE_THETA ** (jnp.arange(0, D_ROPE, 2, dtype=F32) / D_ROPE))
    ang = jnp.arange(S, dtype=F32)[:, None] * inv[None, :]
    cos, sin = jnp.cos(ang), jnp.sin(ang)
    return jnp.concatenate([cos, cos, -sin, sin], axis=-1)


def _swap_halves(w):
    return jnp.concatenate([w[..., D_ROPE // 2:], w[..., :D_ROPE // 2]], axis=-1)


def _tiles(S):
    mla_elems = 4 * 1024 * 1024
    tq_mla = max(256, min(S, mla_elems // S))
    ck_mla = S // 2
    return dict(ts=512, tq_swa=512, tq_mla=tq_mla, ck_mla=ck_mla, ts_out=512, sub_out=2,
                tm_ffn=min(1024, S), ck_ffn=512)


def kernel(x_prompt, x_sample, rel_bias, w_in, sink, q_norm_g, w_uq, kv_norm_g, w_ukv, w_o,
           ln1_g, ln1_b, w_up, conv_w, conv_b, w_down, ln2_g, ln2_b):
    depth = w_in.shape[0]
    alpha = (2 * depth) ** 0.25
    in_cols = w_in.shape[-1]
    w_in_x = jnp.concatenate([w_in, _swap_halves(w_in[..., in_cols - D_ROPE:])], axis=-1).astype(BF16)
    wq = w_uq.reshape(depth, Q_LORA, N_HEADS_B, D_NOPE + D_ROPE)
    w_uq_x = jnp.concatenate([wq, _swap_halves(wq[..., D_NOPE:])], axis=-1)
    w_uq_x = w_uq_x.reshape(depth, Q_LORA, N_HEADS_B * QK_B).astype(BF16)
    w_ukv_b = w_ukv.astype(BF16)
    w_o_b = w_o.astype(BF16)
    w_up_b = w_up.astype(BF16)
    w_down_b = w_down.astype(BF16)
    row = lambda p: p[:, None, :]
    gq, gkv = row(q_norm_g), row(kv_norm_g)
    g1, b1, g2, b2 = row(ln1_g), row(ln1_b), row(ln2_g), row(ln2_b)
    cb = row(conv_b)
    bias = _window_bias(rel_bias)

    def trunk(x):
        S = x.shape[1]
        t = _tiles(S)
        cs = _rope_table(S)
        for l in range(depth):
            qa, ka, vat, qb, kb, vt = _in_proj(x, w_in_x, gq, gkv, w_uq_x, w_ukv_b, cs, layer=l, ts=t["ts"])
            oa = _swa(sink, qa, ka, vat, bias, layer=l, tq=t["tq_swa"])
            ob = _mla(qb, kb, vt, tq=t["tq_mla"], ck=t["ck_mla"])
            x = _out_ln(oa, ob, x, w_o_b, g1, b1, layer=l, alpha=alpha, ts=t["ts_out"], n_sub=t["sub_out"])
            x = _ffn(x, w_up_b, conv_w, cb, w_down_b, g2, b2, layer=l, alpha=alpha,
                     tm=t["tm_ffn"], ck=t["ck_ffn"])
        return x

    return trunk(x_prompt), trunk(x_sample)
```

```python
import functools
import math

import jax
import jax.numpy as jnp
from jax import lax
from jax.experimental import pallas as pl
from jax.experimental.pallas import tpu as pltpu

HEAD_DIM = 128
N_HEADS_A = 8
N_KV_A = 2
GROUP_A = N_HEADS_A // N_KV_A
WINDOW = 128
BLOCK = 128
N_BUCKETS = 32
MAX_DISTANCE = 128
N_HEADS_B = 8
Q_LORA = 512
KV_LORA = 256
D_NOPE = 128
D_ROPE = 64
D_V = 128
ROPE_THETA = 10000.0
LN_EPS = 1e-5
RMS_EPS = 1e-6
NEG_INF = -1e30
LOG2E = math.log2(math.e)

QA_COLS = N_HEADS_A * HEAD_DIM
KA_COLS = N_KV_A * HEAD_DIM
QK_B = D_NOPE + 2 * D_ROPE
KV_B = D_NOPE + D_V

LANES = 128
V7X_VMEM_BYTES = 64 * 1024 * 1024
VMEM_LIMIT = V7X_VMEM_BYTES - 3 * 1024 * 1024

BF16 = jnp.bfloat16
F32 = jnp.float32


def _cparams(semantics):
    return pltpu.CompilerParams(dimension_semantics=semantics, vmem_limit_bytes=VMEM_LIMIT)


def _layer_spec(arr, layer, n_grid):
    zeros = (0,) * (arr.ndim - 1)
    index_map = {2: lambda a, b: (layer,) + zeros, 3: lambda a, b, c: (layer,) + zeros}[n_grid]
    return pl.BlockSpec((None,) + arr.shape[1:], index_map, pipeline_mode=pl.Buffered(1))


def _in_proj_kernel(x_ref, w_in_ref, gq_ref, gkv_ref, w_uq_ref, w_ukv_ref, cs_ref,
                    qa_ref, ka_ref, vat_ref, qb_ref, kb_ref, vt_ref, *, scale_a, scale_b):
    xb = x_ref[...].astype(BF16)
    h = jnp.dot(xb, w_in_ref[...], preferred_element_type=F32)
    c0 = 0
    qa_ref[...] = (h[:, c0:c0 + QA_COLS] * scale_a).astype(BF16)
    c0 += QA_COLS
    ka_ref[...] = h[:, c0:c0 + KA_COLS].astype(BF16)
    c0 += KA_COLS
    for g in range(N_KV_A):
        vat_ref[g] = h[:, c0 + g * HEAD_DIM:c0 + (g + 1) * HEAD_DIM].T.astype(BF16)
    c0 += KA_COLS
    cq = h[:, c0:c0 + Q_LORA]
    c0 += Q_LORA
    ckv = h[:, c0:c0 + KV_LORA]
    c0 += KV_LORA
    kr_pair = h[:, c0:c0 + 2 * D_ROPE]

    cqn = cq * lax.rsqrt(jnp.mean(cq * cq, axis=-1, keepdims=True) + RMS_EPS) * gq_ref[...]
    ckvn = ckv * lax.rsqrt(jnp.mean(ckv * ckv, axis=-1, keepdims=True) + RMS_EPS) * gkv_ref[...]
    qb = jnp.dot(cqn.astype(BF16), w_uq_ref[...], preferred_element_type=F32)
    kvb = jnp.dot(ckvn.astype(BF16), w_ukv_ref[...], preferred_element_type=F32)

    cs = cs_ref[...]
    lane = lax.broadcasted_iota(jnp.int32, cs.shape, 1)
    kt = kr_pair * cs
    kr = kt + pltpu.roll(kt, D_ROPE, 1)
    kr = jnp.where(lane < D_ROPE, kr, 0.0).astype(BF16)
    for hd in range(N_HEADS_B):
        q0 = hd * QK_B
        k0 = hd * KV_B
        qt = qb[:, q0 + D_NOPE:q0 + QK_B] * cs
        qr = qt + pltpu.roll(qt, D_ROPE, 1)
        qb_ref[hd, :, 0:D_NOPE] = (qb[:, q0:q0 + D_NOPE] * scale_b).astype(BF16)
        qb_ref[hd, :, D_NOPE:QK_B] = (qr * scale_b).astype(BF16)
        kb_ref[hd, :, 0:D_NOPE] = kvb[:, k0:k0 + D_NOPE].astype(BF16)
        kb_ref[hd, :, D_NOPE:QK_B] = kr
        vt_ref[hd] = kvb[:, k0 + D_NOPE:k0 + KV_B].T.astype(BF16)


def _in_proj(x, w_in, gq, gkv, w_uq, w_ukv, cs, *, layer, ts):
    B, S, D = x.shape
    scale_a = HEAD_DIM ** -0.5 * LOG2E
    scale_b = (D_NOPE + D_ROPE) ** -0.5 * LOG2E
    grid = (B, S // ts)
    out_shape = (
        jax.ShapeDtypeStruct((B, S, QA_COLS), BF16),
        jax.ShapeDtypeStruct((B, S, KA_COLS), BF16),
        jax.ShapeDtypeStruct((B, N_KV_A, HEAD_DIM, S), BF16),
        jax.ShapeDtypeStruct((B, N_HEADS_B, S, QK_B), BF16),
        jax.ShapeDtypeStruct((B, N_HEADS_B, S, QK_B), BF16),
        jax.ShapeDtypeStruct((B, N_HEADS_B, D_V, S), BF16),
    )
    return pl.pallas_call(
        functools.partial(_in_proj_kernel, scale_a=scale_a, scale_b=scale_b),
        grid=grid,
        in_specs=[
            pl.BlockSpec((None, ts, D), lambda b, i: (b, i, 0)),
            _layer_spec(w_in, layer, 2),
            _layer_spec(gq, layer, 2),
            _layer_spec(gkv, layer, 2),
            _layer_spec(w_uq, layer, 2),
            _layer_spec(w_ukv, layer, 2),
            pl.BlockSpec((ts, LANES), lambda b, i: (i, 0)),
        ],
        out_specs=(
            pl.BlockSpec((None, ts, QA_COLS), lambda b, i: (b, i, 0)),
            pl.BlockSpec((None, ts, KA_COLS), lambda b, i: (b, i, 0)),
            pl.BlockSpec((None, N_KV_A, HEAD_DIM, ts), lambda b, i: (b, 0, 0, i)),
            pl.BlockSpec((None, N_HEADS_B, ts, QK_B), lambda b, i: (b, 0, i, 0)),
            pl.BlockSpec((None, N_HEADS_B, ts, QK_B), lambda b, i: (b, 0, i, 0)),
            pl.BlockSpec((None, N_HEADS_B, D_V, ts), lambda b, i: (b, 0, 0, i)),
        ),
        out_shape=out_shape,
        compiler_params=_cparams(("parallel", "parallel")),
        name="in_proj",
    )(x, w_in, gq, gkv, w_uq, w_ukv, cs)


def _swa_kernel(sink_ref, q_ref, k_ref, vt_ref, bias_ref, o_ref, *, layer, tq):
    S = k_ref.shape[0]
    nb = S // BLOCK
    qi = pl.program_id(1)
    blocks = [(j, g) for j in range(tq // BLOCK) for g in range(N_KV_A)]

    def window(j):
        n = qi * (tq // BLOCK) + j
        case = jnp.where(n == 0, 0, jnp.where(n == nb - 1, 2, 1))
        rows = [pl.multiple_of(jnp.maximum(n - 1, 0) * BLOCK, BLOCK), pl.multiple_of(n * BLOCK, BLOCK),
                pl.multiple_of(jnp.minimum(n + 1, nb - 1) * BLOCK, BLOCK)]
        return case, rows

    def scores(j, g):
        case, rows = window(j)
        cols = slice(g * HEAD_DIM, (g + 1) * HEAD_DIM)
        kw = jnp.concatenate([k_ref[pl.ds(r, BLOCK), cols] for r in rows], axis=0)
        qs = jnp.concatenate([q_ref[j * BLOCK:(j + 1) * BLOCK, hd * HEAD_DIM:(hd + 1) * HEAD_DIM]
                              for hd in range(g * GROUP_A, (g + 1) * GROUP_A)], axis=0)
        s = lax.dot_general(kw, qs, (((1,), (1,)), ((), ())), preferred_element_type=F32)
        return s + bias_ref[case, g]

    s_next = scores(*blocks[0])
    for idx, (j, g) in enumerate(blocks):
        s = s_next
        if idx + 1 < len(blocks):
            s_next = scores(*blocks[idx + 1])
        _, rows = window(j)
        heads = range(g * GROUP_A, (g + 1) * GROUP_A)
        vwt = jnp.concatenate([vt_ref[g, :, pl.ds(r, BLOCK)] for r in rows], axis=1)
        sink = jnp.concatenate([jnp.full((1, BLOCK), sink_ref[layer, hd] * LOG2E, F32) for hd in heads], axis=1)
        m = jnp.maximum(jnp.max(s, axis=0, keepdims=True), sink)
        p = jnp.exp2(s - m)
        l = jnp.sum(p, axis=0, keepdims=True) + jnp.exp2(sink - m)
        o = jnp.dot(vwt, p.astype(BF16), preferred_element_type=F32) * (1.0 / l)
        for t, hd in enumerate(heads):
            o_ref[j * BLOCK:(j + 1) * BLOCK, hd * HEAD_DIM:(hd + 1) * HEAD_DIM] = (
                o[:, t * BLOCK:(t + 1) * BLOCK].T.astype(o_ref.dtype))


def _swa(sink, qa, ka, vat, bias, *, layer, tq):
    B, S, _ = qa.shape
    grid = (B, S // tq)
    return pl.pallas_call(
        functools.partial(_swa_kernel, layer=layer, tq=tq),
        grid=grid,
        in_specs=[
            pl.BlockSpec(memory_space=pltpu.SMEM),
            pl.BlockSpec((None, tq, QA_COLS), lambda b, i: (b, i, 0)),
            pl.BlockSpec((None, S, KA_COLS), lambda b, i: (b, 0, 0)),
            pl.BlockSpec((None, N_KV_A, HEAD_DIM, S), lambda b, i: (b, 0, 0, 0)),
            pl.BlockSpec(bias.shape, lambda b, i: (0, 0, 0, 0)),
        ],
        out_specs=pl.BlockSpec((None, tq, QA_COLS), lambda b, i: (b, i, 0)),
        out_shape=jax.ShapeDtypeStruct((B, S, QA_COLS), BF16),
        compiler_params=_cparams(("parallel", "parallel")),
        name="swa",
    )(sink, qa, ka, vat, bias)


def _mla_kernel(q_ref, k_ref, vt_ref, o_ref, s0_scr, s1_scr, m0_scr, m1_scr):
    t = pl.program_id(0)
    n_tiles = pl.num_programs(0) - 1

    def score_phase(s_scr, m_scr):
        s = lax.dot_general(k_ref[...], q_ref[...], (((1,), (1,)), ((), ())),
                            preferred_element_type=F32)
        s_scr[...] = s
        m_scr[...] = jnp.max(s, axis=0, keepdims=True)

    def value_phase(s_scr, m_scr):
        p = jnp.exp2(s_scr[...] - m_scr[...])
        l = jnp.sum(p, axis=0, keepdims=True)
        o = jnp.dot(vt_ref[...], p.astype(BF16), preferred_element_type=F32) * (1.0 / l)
        o_ref[...] = o.T.astype(o_ref.dtype)

    bufs = ((s0_scr, m0_scr), (s1_scr, m1_scr))
    for parity in range(2):
        mine, other = bufs[parity], bufs[1 - parity]

        @pl.when(jnp.logical_and(t % 2 == parity, t == 0))
        def _():
            score_phase(*mine)

        @pl.when(jnp.logical_and(t % 2 == parity, jnp.logical_and(t > 0, t < n_tiles)))
        def _():
            score_phase(*mine)
            value_phase(*other)

        @pl.when(jnp.logical_and(t % 2 == parity, t == n_tiles))
        def _():
            value_phase(*other)


def _mla(qb, kb, vt, *, tq):
    B, H, S, _ = qb.shape
    nq = S // tq
    n_tiles = B * H * nq

    def tile(x):
        return x // (H * nq), (x // nq) % H, x % nq

    def cur(t):
        return tile(jnp.minimum(t, n_tiles - 1))

    def prev(t):
        return tile(jnp.maximum(t - 1, 0))

    return pl.pallas_call(
        _mla_kernel,
        grid=(n_tiles + 1,),
        in_specs=[
            pl.BlockSpec((None, None, tq, QK_B), lambda t: (cur(t)[0], cur(t)[1], cur(t)[2], 0)),
            pl.BlockSpec((None, None, S, QK_B), lambda t: (cur(t)[0], cur(t)[1], 0, 0)),
            pl.BlockSpec((None, None, D_V, S), lambda t: (prev(t)[0], prev(t)[1], 0, 0)),
        ],
        out_specs=pl.BlockSpec((None, tq, D_V), lambda t: (prev(t)[0], prev(t)[2], prev(t)[1])),
        out_shape=jax.ShapeDtypeStruct((B, S, H * D_V), BF16),
        scratch_shapes=[pltpu.VMEM((S, tq), F32), pltpu.VMEM((S, tq), F32),
                        pltpu.VMEM((1, tq), F32), pltpu.VMEM((1, tq), F32)],
        compiler_params=_cparams(("arbitrary",)),
        name="mla",
    )(qb, kb, vt)


def _layer_norm(y, g, b):
    mu = jnp.mean(y, axis=-1, keepdims=True)
    d = y - mu
    var = jnp.mean(d * d, axis=-1, keepdims=True)
    return d * lax.rsqrt(var + LN_EPS) * g + b


def _out_ln_kernel(oa_ref, ob_ref, x_ref, wo_ref, g_ref, b_ref, o_ref, *, alpha, n_sub):
    ts = x_ref.shape[0]
    rs = ts // n_sub
    for r in range(n_sub):
        rows = slice(r * rs, (r + 1) * rs)
        o_cat = jnp.concatenate([oa_ref[rows, :], ob_ref[rows, :]], axis=-1)
        attn = jnp.dot(o_cat, wo_ref[...], preferred_element_type=F32)
        y = alpha * x_ref[rows, :] + attn
        o_ref[rows, :] = _layer_norm(y, g_ref[...], b_ref[...])


def _out_ln(oa, ob, x, w_o, g, b, *, layer, alpha, ts, n_sub):
    B, S, D = x.shape
    na, nb = oa.shape[-1], ob.shape[-1]
    grid = (B, S // ts)
    return pl.pallas_call(
        functools.partial(_out_ln_kernel, alpha=alpha, n_sub=n_sub),
        grid=grid,
        in_specs=[
            pl.BlockSpec((None, ts, na), lambda b_, i: (b_, i, 0)),
            pl.BlockSpec((None, ts, nb), lambda b_, i: (b_, i, 0)),
            pl.BlockSpec((None, ts, D), lambda b_, i: (b_, i, 0)),
            _layer_spec(w_o, layer, 2),
            _layer_spec(g, layer, 2),
            _layer_spec(b, layer, 2),
        ],
        out_specs=pl.BlockSpec((None, ts, D), lambda b_, i: (b_, i, 0)),
        out_shape=jax.ShapeDtypeStruct((B, S, D), F32),
        compiler_params=_cparams(("parallel", "parallel")),
        name="out_ln",
    )(oa, ob, x, w_o, g, b)


HALO = 8


def _dot_row_halves(a, w):
    half = a.shape[0] // 2 // 16 * 16
    return jnp.concatenate([jnp.dot(a[:half], w, preferred_element_type=F32),
                            jnp.dot(a[half:], w, preferred_element_type=F32)], axis=0)


def _ffn_kernel(xp_ref, x_ref, xn_ref, wg_ref, wv_ref, cwg_ref, cwv_ref, cbg_ref, cbv_ref, wd_ref,
                g_ref, b_ref, o_ref, xb_scr, act_scr, *, alpha):
    i = pl.program_id(1)
    c = pl.program_id(2)
    nt = pl.num_programs(1)
    nck = pl.num_programs(2) - 1
    tm = x_ref.shape[0]

    def up_phase():
        xb = xb_scr[...]
        row = lax.broadcasted_iota(jnp.int32, (tm, 1), 0)
        has_prev = (i > 0).astype(F32)
        has_next = (i < nt - 1).astype(F32)

        def conv_branch(w_ref, cw_ref, cb_ref):
            u_ext = _dot_row_halves(xb, w_ref[...])
            u = u_ext[0:tm]
            u_prev = u_ext[tm + HALO - 1:tm + HALO] * has_prev
            u_next = u_ext[tm + HALO:tm + HALO + 1] * has_next
            u_dn = jnp.where(row == 0, u_prev, pltpu.roll(u, 1, 0))
            u_up = jnp.where(row == tm - 1, u_next, pltpu.roll(u, tm - 1, 0))
            return u_dn * cw_ref[0:1, :] + u * cw_ref[1:2, :] + u_up * cw_ref[2:3, :] + cb_ref[...]

        gate = conv_branch(wg_ref, cwg_ref, cbg_ref)
        val = conv_branch(wv_ref, cwv_ref, cbv_ref)
        act_scr[c % 2] = (gate * (1.0 / (1.0 + jnp.exp(-gate))) * val).astype(BF16)

    def down_part():
        return _dot_row_halves(act_scr[(c + 1) % 2], wd_ref[...])

    @pl.when(c == 0)
    def _():
        xb_scr[0:tm, :] = x_ref[...].astype(BF16)
        xb_scr[tm:tm + 2 * HALO, :] = jnp.concatenate([xp_ref[...], xn_ref[...]], axis=0).astype(BF16)
        o_ref[...] = jnp.zeros_like(o_ref)
        up_phase()

    @pl.when(jnp.logical_and(c > 0, c < nck))
    def _():
        up_phase()
        o_ref[...] += down_part()

    @pl.when(c == nck)
    def _():
        y = alpha * x_ref[...] + (o_ref[...] + down_part())
        o_ref[...] = _layer_norm(y, g_ref[...], b_ref[...])


def _ffn(x, w_up, conv_w, conv_b, w_down, g, b, *, layer, alpha, tm, ck):
    B, S, D = x.shape
    F = w_down.shape[1]
    nck = F // ck
    nh = tm // HALO
    grid = (B, S // tm, nck + 1)
    last_halo = S // HALO - 1
    up_idx = lambda c: jnp.minimum(c, nck - 1)
    dn_idx = lambda c: jnp.maximum(c - 1, 0)
    return pl.pallas_call(
        functools.partial(_ffn_kernel, alpha=alpha),
        grid=grid,
        in_specs=[
            pl.BlockSpec((None, HALO, D), lambda b_, i, c: (b_, jnp.maximum(i * nh - 1, 0), 0)),
            pl.BlockSpec((None, tm, D), lambda b_, i, c: (b_, i, 0)),
            pl.BlockSpec((None, HALO, D), lambda b_, i, c: (b_, jnp.minimum((i + 1) * nh, last_halo), 0)),
            pl.BlockSpec((None, D, ck), lambda b_, i, c: (layer, 0, up_idx(c))),
            pl.BlockSpec((None, D, ck), lambda b_, i, c: (layer, 0, nck + up_idx(c))),
            pl.BlockSpec((None, 3, ck), lambda b_, i, c: (layer, 0, up_idx(c))),
            pl.BlockSpec((None, 3, ck), lambda b_, i, c: (layer, 0, nck + up_idx(c))),
            pl.BlockSpec((None, 1, ck), lambda b_, i, c: (layer, 0, up_idx(c))),
            pl.BlockSpec((None, 1, ck), lambda b_, i, c: (layer, 0, nck + up_idx(c))),
            pl.BlockSpec((None, ck, D), lambda b_, i, c: (layer, dn_idx(c), 0)),
            _layer_spec(g, layer, 3),
            _layer_spec(b, layer, 3),
        ],
        out_specs=pl.BlockSpec((None, tm, D), lambda b_, i, c: (b_, i, 0)),
        out_shape=jax.ShapeDtypeStruct((B, S, D), F32),
        scratch_shapes=[pltpu.VMEM((tm + 2 * HALO, D), BF16), pltpu.VMEM((2, tm, ck), BF16)],
        compiler_params=_cparams(("parallel", "parallel", "arbitrary")),
        name="ffn",
    )(x, x, x, w_up, w_up, conv_w, conv_w, conv_b, conv_b, w_down, g, b)


def _t5_bucket(rel):
    half = N_BUCKETS // 2
    max_exact = half // 2
    ret = (rel > 0).astype(jnp.int32) * half
    n = jnp.abs(rel)
    large = max_exact + (jnp.log(jnp.maximum(n, 1).astype(jnp.float32) / max_exact)
                         / math.log(MAX_DISTANCE / max_exact) * (half - max_exact)).astype(jnp.int32)
    large = jnp.minimum(large, half - 1)
    return ret + jnp.where(n < max_exact, n, large)


def _window_bias(rel_bias):
    qi = jnp.arange(BLOCK, dtype=jnp.int32)
    ki = jnp.arange(3 * BLOCK, dtype=jnp.int32)
    rel = ki[None, :] - BLOCK - qi[:, None]
    bucket = _t5_bucket(rel)
    table = rel_bias.astype(F32).T
    hit = bucket[None, :, :, None] == jnp.arange(N_BUCKETS, dtype=jnp.int32)
    bias = jnp.sum(jnp.where(hit, table[:, None, None, :], 0.0), axis=-1)
    base = jnp.where((jnp.abs(rel) <= WINDOW)[None], bias, NEG_INF)
    first = jnp.where((ki >= BLOCK)[None, None, :], base, NEG_INF)
    last = jnp.where((ki < 2 * BLOCK)[None, None, :], base, NEG_INF)
    tables = jnp.stack([first, base, last]) * LOG2E
    tables = tables.reshape(3, N_KV_A, GROUP_A, BLOCK, 3 * BLOCK).transpose(0, 1, 4, 2, 3)
    return tables.reshape(3, N_KV_A, 3 * BLOCK, GROUP_A * BLOCK)


def _rope_table(S):
    inv = 1.0 / (ROPE_THETA ** (jnp.arange(0, D_ROPE, 2, dtype=F32) / D_ROPE))
    ang = jnp.arange(S, dtype=F32)[:, None] * inv[None, :]
    cos, sin = jnp.cos(ang), jnp.sin(ang)
    return jnp.concatenate([cos, cos, -sin, sin], axis=-1)


def _swap_halves(w):
    return jnp.concatenate([w[..., D_ROPE // 2:], w[..., :D_ROPE // 2]], axis=-1)


def _tiles(S):
    mla_elems = 4 * 1024 * 1024
    tq_mla = max(256, min(S, mla_elems // S))
    return dict(ts=512, tq_swa=min(2048, S), tq_mla=tq_mla, ts_out=512, sub_out=2,
                tm_ffn=min(1024, S), ck_ffn=512)


def kernel(x_prompt, x_sample, rel_bias, w_in, sink, q_norm_g, w_uq, kv_norm_g, w_ukv, w_o,
           ln1_g, ln1_b, w_up, conv_w, conv_b, w_down, ln2_g, ln2_b):
    depth = w_in.shape[0]
    alpha = (2 * depth) ** 0.25
    in_cols = w_in.shape[-1]
    w_in_x = jnp.concatenate([w_in, _swap_halves(w_in[..., in_cols - D_ROPE:])], axis=-1).astype(BF16)
    wq = w_uq.reshape(depth, Q_LORA, N_HEADS_B, D_NOPE + D_ROPE)
    w_uq_x = jnp.concatenate([wq, _swap_halves(wq[..., D_NOPE:])], axis=-1)
    w_uq_x = w_uq_x.reshape(depth, Q_LORA, N_HEADS_B * QK_B).astype(BF16)
    w_ukv_b = w_ukv.astype(BF16)
    w_o_b = w_o.astype(BF16)
    w_up_b = w_up.astype(BF16)
    w_down_b = w_down.astype(BF16)
    row = lambda p: p[:, None, :]
    gq, gkv = row(q_norm_g), row(kv_norm_g)
    g1, b1, g2, b2 = row(ln1_g), row(ln1_b), row(ln2_g), row(ln2_b)
    cb = row(conv_b)
    bias = _window_bias(rel_bias)

    def trunk(x):
        S = x.shape[1]
        t = _tiles(S)
        cs = _rope_table(S)
        for l in range(depth):
            qa, ka, vat, qb, kb, vt = _in_proj(x, w_in_x, gq, gkv, w_uq_x, w_ukv_b, cs, layer=l, ts=t["ts"])
            oa = _swa(sink, qa, ka, vat, bias, layer=l, tq=t["tq_swa"])
            ob = _mla(qb, kb, vt, tq=t["tq_mla"])
            x = _out_ln(oa, ob, x, w_o_b, g1, b1, layer=l, alpha=alpha, ts=t["ts_out"], n_sub=t["sub_out"])
            x = _ffn(x, w_up_b, conv_w, cb, w_down_b, g2, b2, layer=l, alpha=alpha,
                     tm=t["tm_ffn"], ck=t["ck_ffn"])
        return x

    return trunk(x_prompt), trunk(x_sample)
```

```python
import functools
import math

import jax
import jax.numpy as jnp
from jax import lax
from jax.experimental import pallas as pl
from jax.experimental.pallas import tpu as pltpu

HEAD_DIM = 128
N_HEADS_A = 8
N_KV_A = 2
GROUP_A = N_HEADS_A // N_KV_A
WINDOW = 128
BLOCK = 128
N_BUCKETS = 32
MAX_DISTANCE = 128
N_HEADS_B = 8
Q_LORA = 512
KV_LORA = 256
D_NOPE = 128
D_ROPE = 64
D_V = 128
ROPE_THETA = 10000.0
LN_EPS = 1e-5
RMS_EPS = 1e-6
NEG_INF = -1e30
LOG2E = math.log2(math.e)

QA_COLS = N_HEADS_A * HEAD_DIM
KA_COLS = N_KV_A * HEAD_DIM
QK_B = D_NOPE + 2 * D_ROPE
KV_B = D_NOPE + D_V

LANES = 128
BF16_SUBLANES = 16
V7X_VMEM_BYTES = 64 * 1024 * 1024
VMEM_LIMIT = V7X_VMEM_BYTES - 3 * 1024 * 1024

BF16 = jnp.bfloat16
F32 = jnp.float32


def _cparams(semantics):
    return pltpu.CompilerParams(dimension_semantics=semantics, vmem_limit_bytes=VMEM_LIMIT)


def _layer_spec(arr, layer, n_grid):
    zeros = (0,) * (arr.ndim - 1)
    index_map = {2: lambda a, b: (layer,) + zeros, 3: lambda a, b, c: (layer,) + zeros}[n_grid]
    return pl.BlockSpec((None,) + arr.shape[1:], index_map, pipeline_mode=pl.Buffered(1))


def _in_proj_kernel(x_ref, w_in_ref, gq_ref, gkv_ref, w_uq_ref, w_ukv_ref, cs_ref,
                    qa_ref, ka_ref, vat_ref, qb_ref, kb_ref, vt_ref, *, scale_a, scale_b):
    xb = x_ref[...].astype(BF16)
    h = jnp.dot(xb, w_in_ref[...], preferred_element_type=F32)
    c0 = 0
    qa_ref[...] = (h[:, c0:c0 + QA_COLS] * scale_a).astype(BF16)
    c0 += QA_COLS
    ka_ref[...] = h[:, c0:c0 + KA_COLS].astype(BF16)
    c0 += KA_COLS
    for g in range(N_KV_A):
        vat_ref[g] = h[:, c0 + g * HEAD_DIM:c0 + (g + 1) * HEAD_DIM].T.astype(BF16)
    c0 += KA_COLS
    cq = h[:, c0:c0 + Q_LORA]
    c0 += Q_LORA
    ckv = h[:, c0:c0 + KV_LORA]
    c0 += KV_LORA
    kr_pair = h[:, c0:c0 + 2 * D_ROPE]

    cqn = cq * lax.rsqrt(jnp.mean(cq * cq, axis=-1, keepdims=True) + RMS_EPS) * gq_ref[...]
    ckvn = ckv * lax.rsqrt(jnp.mean(ckv * ckv, axis=-1, keepdims=True) + RMS_EPS) * gkv_ref[...]
    qb = jnp.dot(cqn.astype(BF16), w_uq_ref[...], preferred_element_type=F32)
    kvb = jnp.dot(ckvn.astype(BF16), w_ukv_ref[...], preferred_element_type=F32)

    cs = cs_ref[...]
    lane = lax.broadcasted_iota(jnp.int32, cs.shape, 1)
    kt = kr_pair * cs
    kr = kt + pltpu.roll(kt, D_ROPE, 1)
    kr = jnp.where(lane < D_ROPE, kr, 0.0).astype(BF16)
    for hd in range(N_HEADS_B):
        q0 = hd * QK_B
        k0 = hd * KV_B
        qt = qb[:, q0 + D_NOPE:q0 + QK_B] * cs
        qr = qt + pltpu.roll(qt, D_ROPE, 1)
        qb_ref[hd, :, 0:D_NOPE] = (qb[:, q0:q0 + D_NOPE] * scale_b).astype(BF16)
        qb_ref[hd, :, D_NOPE:QK_B] = (qr * scale_b).astype(BF16)
        kb_ref[hd, :, 0:D_NOPE] = kvb[:, k0:k0 + D_NOPE].astype(BF16)
        kb_ref[hd, :, D_NOPE:QK_B] = kr
        vt_ref[hd] = kvb[:, k0 + D_NOPE:k0 + KV_B].T.astype(BF16)


def _in_proj(x, w_in, gq, gkv, w_uq, w_ukv, cs, *, layer, ts):
    B, S, D = x.shape
    scale_a = HEAD_DIM ** -0.5 * LOG2E
    scale_b = (D_NOPE + D_ROPE) ** -0.5 * LOG2E
    grid = (B, S // ts)
    out_shape = (
        jax.ShapeDtypeStruct((B, S, QA_COLS), BF16),
        jax.ShapeDtypeStruct((B, S, KA_COLS), BF16),
        jax.ShapeDtypeStruct((B, N_KV_A, HEAD_DIM, S), BF16),
        jax.ShapeDtypeStruct((B, N_HEADS_B, S, QK_B), BF16),
        jax.ShapeDtypeStruct((B, N_HEADS_B, S, QK_B), BF16),
        jax.ShapeDtypeStruct((B, N_HEADS_B, D_V, S), BF16),
    )
    return pl.pallas_call(
        functools.partial(_in_proj_kernel, scale_a=scale_a, scale_b=scale_b),
        grid=grid,
        in_specs=[
            pl.BlockSpec((None, ts, D), lambda b, i: (b, i, 0)),
            _layer_spec(w_in, layer, 2),
            _layer_spec(gq, layer, 2),
            _layer_spec(gkv, layer, 2),
            _layer_spec(w_uq, layer, 2),
            _layer_spec(w_ukv, layer, 2),
            pl.BlockSpec((ts, LANES), lambda b, i: (i, 0)),
        ],
        out_specs=(
            pl.BlockSpec((None, ts, QA_COLS), lambda b, i: (b, i, 0)),
            pl.BlockSpec((None, ts, KA_COLS), lambda b, i: (b, i, 0)),
            pl.BlockSpec((None, N_KV_A, HEAD_DIM, ts), lambda b, i: (b, 0, 0, i)),
            pl.BlockSpec((None, N_HEADS_B, ts, QK_B), lambda b, i: (b, 0, i, 0)),
            pl.BlockSpec((None, N_HEADS_B, ts, QK_B), lambda b, i: (b, 0, i, 0)),
            pl.BlockSpec((None, N_HEADS_B, D_V, ts), lambda b, i: (b, 0, 0, i)),
        ),
        out_shape=out_shape,
        compiler_params=_cparams(("parallel", "parallel")),
        name="in_proj",
    )(x, w_in, gq, gkv, w_uq, w_ukv, cs)


def _swa_kernel(sink_ref, q_ref, k_ref, vt_ref, bias_ref, o_ref, *, layer, tq):
    S = k_ref.shape[0]
    nb = S // BLOCK
    qi = pl.program_id(1)
    blocks = [(j, g) for j in range(tq // BLOCK) for g in range(N_KV_A)]

    def window(j):
        n = qi * (tq // BLOCK) + j
        case = jnp.where(n == 0, 0, jnp.where(n == nb - 1, 2, 1))
        rows = [pl.multiple_of(jnp.maximum(n - 1, 0) * BLOCK, BLOCK), pl.multiple_of(n * BLOCK, BLOCK),
                pl.multiple_of(jnp.minimum(n + 1, nb - 1) * BLOCK, BLOCK)]
        return case, rows

    def scores(j, g):
        case, rows = window(j)
        cols = slice(g * HEAD_DIM, (g + 1) * HEAD_DIM)
        kw = jnp.concatenate([k_ref[pl.ds(r, BLOCK), cols] for r in rows], axis=0)
        qs = jnp.concatenate([q_ref[j * BLOCK:(j + 1) * BLOCK, hd * HEAD_DIM:(hd + 1) * HEAD_DIM]
                              for hd in range(g * GROUP_A, (g + 1) * GROUP_A)], axis=0)
        s = lax.dot_general(kw, qs, (((1,), (1,)), ((), ())), preferred_element_type=F32)
        return s + bias_ref[case, g]

    s_next = scores(*blocks[0])
    for idx, (j, g) in enumerate(blocks):
        s = s_next
        if idx + 1 < len(blocks):
            s_next = scores(*blocks[idx + 1])
        _, rows = window(j)
        heads = range(g * GROUP_A, (g + 1) * GROUP_A)
        vwt = jnp.concatenate([vt_ref[g, :, pl.ds(r, BLOCK)] for r in rows], axis=1)
        sink = jnp.concatenate([jnp.full((1, BLOCK), sink_ref[layer, hd] * LOG2E, F32) for hd in heads], axis=1)
        m = jnp.maximum(jnp.max(s, axis=0, keepdims=True), sink)
        p = jnp.exp2(s - m)
        l = jnp.sum(p, axis=0, keepdims=True) + jnp.exp2(sink - m)
        o = jnp.dot(vwt, p.astype(BF16), preferred_element_type=F32) * (1.0 / l)
        for t, hd in enumerate(heads):
            o_ref[j * BLOCK:(j + 1) * BLOCK, hd * HEAD_DIM:(hd + 1) * HEAD_DIM] = (
                o[:, t * BLOCK:(t + 1) * BLOCK].T.astype(o_ref.dtype))


def _swa(sink, qa, ka, vat, bias, *, layer, tq):
    B, S, _ = qa.shape
    grid = (B, S // tq)
    return pl.pallas_call(
        functools.partial(_swa_kernel, layer=layer, tq=tq),
        grid=grid,
        in_specs=[
            pl.BlockSpec(memory_space=pltpu.SMEM),
            pl.BlockSpec((None, tq, QA_COLS), lambda b, i: (b, i, 0)),
            pl.BlockSpec((None, S, KA_COLS), lambda b, i: (b, 0, 0)),
            pl.BlockSpec((None, N_KV_A, HEAD_DIM, S), lambda b, i: (b, 0, 0, 0)),
            pl.BlockSpec(bias.shape, lambda b, i: (0, 0, 0, 0)),
        ],
        out_specs=pl.BlockSpec((None, tq, QA_COLS), lambda b, i: (b, i, 0)),
        out_shape=jax.ShapeDtypeStruct((B, S, QA_COLS), BF16),
        compiler_params=_cparams(("parallel", "parallel")),
        name="swa",
    )(sink, qa, ka, vat, bias)


def _mla_kernel(q_ref, k_ref, vt_ref, o_ref, s0_scr, s1_scr, m0_scr, m1_scr):
    t = pl.program_id(0)
    n_tiles = pl.num_programs(0) - 1

    def score_phase(s_scr, m_scr):
        s = lax.dot_general(k_ref[...], q_ref[...], (((1,), (1,)), ((), ())),
                            preferred_element_type=F32)
        s_scr[...] = s
        m_scr[...] = jnp.max(s, axis=0, keepdims=True)

    def value_phase(s_scr, m_scr):
        p = jnp.exp2(s_scr[...] - m_scr[...])
        l = jnp.sum(p, axis=0, keepdims=True)
        o = jnp.dot(vt_ref[...], p.astype(BF16), preferred_element_type=F32) * (1.0 / l)
        o_ref[...] = o.T.astype(o_ref.dtype)

    bufs = ((s0_scr, m0_scr), (s1_scr, m1_scr))
    for parity in range(2):
        mine, other = bufs[parity], bufs[1 - parity]

        @pl.when(jnp.logical_and(t % 2 == parity, t == 0))
        def _():
            score_phase(*mine)

        @pl.when(jnp.logical_and(t % 2 == parity, jnp.logical_and(t > 0, t < n_tiles)))
        def _():
            score_phase(*mine)
            value_phase(*other)

        @pl.when(jnp.logical_and(t % 2 == parity, t == n_tiles))
        def _():
            value_phase(*other)


def _mla(qb, kb, vt, *, tq):
    B, H, S, _ = qb.shape
    nq = S // tq
    n_tiles = B * H * nq

    def tile(x):
        return x // (H * nq), (x // nq) % H, x % nq

    def cur(t):
        return tile(jnp.minimum(t, n_tiles - 1))

    def prev(t):
        return tile(jnp.maximum(t - 1, 0))

    return pl.pallas_call(
        _mla_kernel,
        grid=(n_tiles + 1,),
        in_specs=[
            pl.BlockSpec((None, None, tq, QK_B), lambda t: (cur(t)[0], cur(t)[1], cur(t)[2], 0)),
            pl.BlockSpec((None, None, S, QK_B), lambda t: (cur(t)[0], cur(t)[1], 0, 0)),
            pl.BlockSpec((None, None, D_V, S), lambda t: (prev(t)[0], prev(t)[1], 0, 0)),
        ],
        out_specs=pl.BlockSpec((None, tq, D_V), lambda t: (prev(t)[0], prev(t)[2], prev(t)[1])),
        out_shape=jax.ShapeDtypeStruct((B, S, H * D_V), BF16),
        scratch_shapes=[pltpu.VMEM((S, tq), F32), pltpu.VMEM((S, tq), F32),
                        pltpu.VMEM((1, tq), F32), pltpu.VMEM((1, tq), F32)],
        compiler_params=_cparams(("arbitrary",)),
        name="mla",
    )(qb, kb, vt)


def _layer_norm(y, g, b):
    mu = jnp.mean(y, axis=-1, keepdims=True)
    d = y - mu
    var = jnp.mean(d * d, axis=-1, keepdims=True)
    return d * lax.rsqrt(var + LN_EPS) * g + b


def _out_ln_kernel(oa_ref, ob_ref, x_ref, wo_ref, g_ref, b_ref, o_ref, *, alpha, n_sub):
    ts = x_ref.shape[0]
    rs = ts // n_sub
    for r in range(n_sub):
        rows = slice(r * rs, (r + 1) * rs)
        o_cat = jnp.concatenate([oa_ref[rows, :], ob_ref[rows, :]], axis=-1)
        attn = jnp.dot(o_cat, wo_ref[...], preferred_element_type=F32)
        y = alpha * x_ref[rows, :] + attn
        o_ref[rows, :] = _layer_norm(y, g_ref[...], b_ref[...])


def _out_ln(oa, ob, x, w_o, g, b, *, layer, alpha, ts, n_sub):
    B, S, D = x.shape
    na, nb = oa.shape[-1], ob.shape[-1]
    grid = (B, S // ts)
    return pl.pallas_call(
        functools.partial(_out_ln_kernel, alpha=alpha, n_sub=n_sub),
        grid=grid,
        in_specs=[
            pl.BlockSpec((None, ts, na), lambda b_, i: (b_, i, 0)),
            pl.BlockSpec((None, ts, nb), lambda b_, i: (b_, i, 0)),
            pl.BlockSpec((None, ts, D), lambda b_, i: (b_, i, 0)),
            _layer_spec(w_o, layer, 2),
            _layer_spec(g, layer, 2),
            _layer_spec(b, layer, 2),
        ],
        out_specs=pl.BlockSpec((None, ts, D), lambda b_, i: (b_, i, 0)),
        out_shape=jax.ShapeDtypeStruct((B, S, D), F32),
        compiler_params=_cparams(("parallel", "parallel")),
        name="out_ln",
    )(oa, ob, x, w_o, g, b)


HALO = 8


def _dot_row_halves(a, w):
    half = a.shape[0] // 2 // BF16_SUBLANES * BF16_SUBLANES
    return jnp.concatenate([jnp.dot(a[:half], w, preferred_element_type=F32),
                            jnp.dot(a[half:], w, preferred_element_type=F32)], axis=0)


def _ffn_kernel(xp_ref, x_ref, xn_ref, wg_ref, wv_ref, cwg_ref, cwv_ref, cbg_ref, cbv_ref, wd_ref,
                g_ref, b_ref, o_ref, xb_scr, act_scr, *, alpha):
    i = pl.program_id(1)
    c = pl.program_id(2)
    nt = pl.num_programs(1)
    nck = pl.num_programs(2) - 1
    tm = x_ref.shape[0]

    def up_phase():
        xb = xb_scr[...]
        row = lax.broadcasted_iota(jnp.int32, (tm, 1), 0)
        has_prev = (i > 0).astype(F32)
        has_next = (i < nt - 1).astype(F32)

        def conv_branch(w_ref, cw_ref, cb_ref):
            u_ext = _dot_row_halves(xb, w_ref[...])
            u = u_ext[0:tm]
            u_prev = u_ext[tm + HALO - 1:tm + HALO] * has_prev
            u_next = u_ext[tm + HALO:tm + HALO + 1] * has_next
            u_dn = jnp.where(row == 0, u_prev, pltpu.roll(u, 1, 0))
            u_up = jnp.where(row == tm - 1, u_next, pltpu.roll(u, tm - 1, 0))
            return u_dn * cw_ref[0:1, :] + u * cw_ref[1:2, :] + u_up * cw_ref[2:3, :] + cb_ref[...]

        gate = conv_branch(wg_ref, cwg_ref, cbg_ref)
        val = conv_branch(wv_ref, cwv_ref, cbv_ref)
        act_scr[c % 2] = (gate * (1.0 / (1.0 + jnp.exp(-gate))) * val).astype(BF16)

    def down_part():
        return _dot_row_halves(act_scr[(c + 1) % 2], wd_ref[...])

    @pl.when(c == 0)
    def _():
        xb_scr[0:tm, :] = x_ref[...].astype(BF16)
        xb_scr[tm:tm + 2 * HALO, :] = jnp.concatenate([xp_ref[...], xn_ref[...]], axis=0).astype(BF16)
        o_ref[...] = jnp.zeros_like(o_ref)
        up_phase()

    @pl.when(jnp.logical_and(c > 0, c < nck))
    def _():
        up_phase()
        o_ref[...] += down_part()

    @pl.when(c == nck)
    def _():
        y = alpha * x_ref[...] + (o_ref[...] + down_part())
        o_ref[...] = _layer_norm(y, g_ref[...], b_ref[...])


def _ffn(x, w_up, conv_w, conv_b, w_down, g, b, *, layer, alpha, tm, ck):
    B, S, D = x.shape
    F = w_down.shape[1]
    nck = F // ck
    nh = tm // HALO
    grid = (B, S // tm, nck + 1)
    last_halo = S // HALO - 1
    up_idx = lambda c: jnp.minimum(c, nck - 1)
    dn_idx = lambda c: jnp.maximum(c - 1, 0)
    return pl.pallas_call(
        functools.partial(_ffn_kernel, alpha=alpha),
        grid=grid,
        in_specs=[
            pl.BlockSpec((None, HALO, D), lambda b_, i, c: (b_, jnp.maximum(i * nh - 1, 0), 0)),
            pl.BlockSpec((None, tm, D), lambda b_, i, c: (b_, i, 0)),
            pl.BlockSpec((None, HALO, D), lambda b_, i, c: (b_, jnp.minimum((i + 1) * nh, last_halo), 0)),
            pl.BlockSpec((None, D, ck), lambda b_, i, c: (layer, 0, up_idx(c))),
            pl.BlockSpec((None, D, ck), lambda b_, i, c: (layer, 0, nck + up_idx(c))),
            pl.BlockSpec((None, 3, ck), lambda b_, i, c: (layer, 0, up_idx(c))),
            pl.BlockSpec((None, 3, ck), lambda b_, i, c: (layer, 0, nck + up_idx(c))),
            pl.BlockSpec((None, 1, ck), lambda b_, i, c: (layer, 0, up_idx(c))),
            pl.BlockSpec((None, 1, ck), lambda b_, i, c: (layer, 0, nck + up_idx(c))),
            pl.BlockSpec((None, ck, D), lambda b_, i, c: (layer, dn_idx(c), 0)),
            _layer_spec(g, layer, 3),
            _layer_spec(b, layer, 3),
        ],
        out_specs=pl.BlockSpec((None, tm, D), lambda b_, i, c: (b_, i, 0)),
        out_shape=jax.ShapeDtypeStruct((B, S, D), F32),
        scratch_shapes=[pltpu.VMEM((tm + 2 * HALO, D), BF16), pltpu.VMEM((2, tm, ck), BF16)],
        compiler_params=_cparams(("parallel", "parallel", "arbitrary")),
        name="ffn",
    )(x, x, x, w_up, w_up, conv_w, conv_w, conv_b, conv_b, w_down, g, b)


def _t5_bucket(rel):
    half = N_BUCKETS // 2
    max_exact = half // 2
    ret = (rel > 0).astype(jnp.int32) * half
    n = jnp.abs(rel)
    large = max_exact + (jnp.log(jnp.maximum(n, 1).astype(jnp.float32) / max_exact)
                         / math.log(MAX_DISTANCE / max_exact) * (half - max_exact)).astype(jnp.int32)
    large = jnp.minimum(large, half - 1)
    return ret + jnp.where(n < max_exact, n, large)


def _window_bias(rel_bias):
    qi = jnp.arange(BLOCK, dtype=jnp.int32)
    ki = jnp.arange(3 * BLOCK, dtype=jnp.int32)
    rel = ki[None, :] - BLOCK - qi[:, None]
    bucket = _t5_bucket(rel)
    table = rel_bias.astype(F32).T
    hit = bucket[None, :, :, None] == jnp.arange(N_BUCKETS, dtype=jnp.int32)
    bias = jnp.sum(jnp.where(hit, table[:, None, None, :], 0.0), axis=-1)
    base = jnp.where((jnp.abs(rel) <= WINDOW)[None], bias, NEG_INF)
    first = jnp.where((ki >= BLOCK)[None, None, :], base, NEG_INF)
    last = jnp.where((ki < 2 * BLOCK)[None, None, :], base, NEG_INF)
    tables = jnp.stack([first, base, last]) * LOG2E
    tables = tables.reshape(3, N_KV_A, GROUP_A, BLOCK, 3 * BLOCK).transpose(0, 1, 4, 2, 3)
    return tables.reshape(3, N_KV_A, 3 * BLOCK, GROUP_A * BLOCK)


def _rope_table(S):
    inv = 1.0 / (ROPE_THETA ** (jnp.arange(0, D_ROPE, 2, dtype=F32) / D_ROPE))
    ang = jnp.arange(S, dtype=F32)[:, None] * inv[None, :]
    cos, sin = jnp.cos(ang), jnp.sin(ang)
    return jnp.concatenate([cos, cos, -sin, sin], axis=-1)


def _swap_halves(w):
    return jnp.concatenate([w[..., D_ROPE // 2:], w[..., :D_ROPE // 2]], axis=-1)


def _tiles(S):
    mla_elems = 4 * 1024 * 1024
    tq_mla = max(2 * LANES, min(S, mla_elems // S))
    return dict(ts=512, tq_swa=min(2048, S), tq_mla=tq_mla, ts_out=512, sub_out=2,
                tm_ffn=min(1024, S), ck_ffn=512)


def kernel(x_prompt, x_sample, rel_bias, w_in, sink, q_norm_g, w_uq, kv_norm_g, w_ukv, w_o,
           ln1_g, ln1_b, w_up, conv_w, conv_b, w_down, ln2_g, ln2_b):
    depth = w_in.shape[0]
    alpha = (2 * depth) ** 0.25
    in_cols = w_in.shape[-1]
    w_in_b = w_in.astype(BF16)
    w_in_x = jnp.concatenate([w_in_b, _swap_halves(w_in_b[..., in_cols - D_ROPE:])], axis=-1)
    wq = w_uq.astype(BF16).reshape(depth, Q_LORA, N_HEADS_B, D_NOPE + D_ROPE)
    w_uq_x = jnp.concatenate([wq, _swap_halves(wq[..., D_NOPE:])], axis=-1)
    w_uq_x = w_uq_x.reshape(depth, Q_LORA, N_HEADS_B * QK_B)
    w_ukv_b = w_ukv.astype(BF16)
    w_o_b = w_o.astype(BF16)
    w_up_b = w_up.astype(BF16)
    w_down_b = w_down.astype(BF16)
    row = lambda p: p[:, None, :]
    gq, gkv = row(q_norm_g), row(kv_norm_g)
    g1, b1, g2, b2 = row(ln1_g), row(ln1_b), row(ln2_g), row(ln2_b)
    cb = row(conv_b)
    bias = _window_bias(rel_bias)

    def trunk(x):
        S = x.shape[1]
        t = _tiles(S)
        assert S >= 2 * BLOCK and all(S % t[k] == 0 for k in ("ts", "tq_swa", "tq_mla", "ts_out", "tm_ffn")), (S, t)
        cs = _rope_table(S)
        for l in range(depth):
            qa, ka, vat, qb, kb, vt = _in_proj(x, w_in_x, gq, gkv, w_uq_x, w_ukv_b, cs, layer=l, ts=t["ts"])
            oa = _swa(sink, qa, ka, vat, bias, layer=l, tq=t["tq_swa"])
            ob = _mla(qb, kb, vt, tq=t["tq_mla"])
            x = _out_ln(oa, ob, x, w_o_b, g1, b1, layer=l, alpha=alpha, ts=t["ts_out"], n_sub=t["sub_out"])
            x = _ffn(x, w_up_b, conv_w, cb, w_down_b, g2, b2, layer=l, alpha=alpha,
                     tm=t["tm_ffn"], ck=t["ck_ffn"])
        return x

    return trunk(x_prompt), trunk(x_sample)
```

```python
import functools
import math

import jax
import jax.numpy as jnp
from jax import lax
from jax.experimental import pallas as pl
from jax.experimental.pallas import tpu as pltpu

HEAD_DIM = 128
N_HEADS_A = 8
N_KV_A = 2
GROUP_A = N_HEADS_A // N_KV_A
WINDOW = 128
BLOCK = 128
N_BUCKETS = 32
MAX_DISTANCE = 128
N_HEADS_B = 8
Q_LORA = 512
KV_LORA = 256
D_NOPE = 128
D_ROPE = 64
D_V = 128
ROPE_THETA = 10000.0
LN_EPS = 1e-5
RMS_EPS = 1e-6
NEG_INF = -1e30
LOG2E = math.log2(math.e)

QA_COLS = N_HEADS_A * HEAD_DIM
KA_COLS = N_KV_A * HEAD_DIM
QK_B = D_NOPE + 2 * D_ROPE
KV_B = D_NOPE + D_V

LANES = 128
BF16_SUBLANES = 16
V7X_VMEM_BYTES = 64 * 1024 * 1024
VMEM_LIMIT = V7X_VMEM_BYTES - 3 * 1024 * 1024

BF16 = jnp.bfloat16
F32 = jnp.float32


def _cparams(semantics):
    return pltpu.CompilerParams(dimension_semantics=semantics, vmem_limit_bytes=VMEM_LIMIT)


def _layer_spec(arr, layer, n_grid):
    zeros = (0,) * (arr.ndim - 1)
    index_map = {2: lambda a, b: (layer,) + zeros, 3: lambda a, b, c: (layer,) + zeros}[n_grid]
    return pl.BlockSpec((None,) + arr.shape[1:], index_map, pipeline_mode=pl.Buffered(1))


def _in_proj_kernel(x_ref, w_in_ref, gq_ref, gkv_ref, w_uq_ref, w_ukv_ref, cs_ref,
                    qa_ref, ka_ref, vat_ref, qb_ref, kb_ref, vt_ref, *, scale_a, scale_b):
    xb = x_ref[...].astype(BF16)
    h = jnp.dot(xb, w_in_ref[...], preferred_element_type=F32)
    c0 = 0
    qa_ref[...] = (h[:, c0:c0 + QA_COLS] * scale_a).astype(BF16)
    c0 += QA_COLS
    ka_ref[...] = h[:, c0:c0 + KA_COLS].astype(BF16)
    c0 += KA_COLS
    for g in range(N_KV_A):
        vat_ref[g] = h[:, c0 + g * HEAD_DIM:c0 + (g + 1) * HEAD_DIM].T.astype(BF16)
    c0 += KA_COLS
    cq = h[:, c0:c0 + Q_LORA]
    c0 += Q_LORA
    ckv = h[:, c0:c0 + KV_LORA]
    c0 += KV_LORA
    kr_pair = h[:, c0:c0 + 2 * D_ROPE]

    cqn = cq * lax.rsqrt(jnp.mean(cq * cq, axis=-1, keepdims=True) + RMS_EPS) * gq_ref[...]
    ckvn = ckv * lax.rsqrt(jnp.mean(ckv * ckv, axis=-1, keepdims=True) + RMS_EPS) * gkv_ref[...]
    qb = jnp.dot(cqn.astype(BF16), w_uq_ref[...], preferred_element_type=F32)
    kvb = jnp.dot(ckvn.astype(BF16), w_ukv_ref[...], preferred_element_type=F32)

    cs = cs_ref[...]
    lane = lax.broadcasted_iota(jnp.int32, cs.shape, 1)
    kt = kr_pair * cs
    kr = kt + pltpu.roll(kt, D_ROPE, 1)
    kr = jnp.where(lane < D_ROPE, kr, 0.0).astype(BF16)
    for hd in range(N_HEADS_B):
        q0 = hd * QK_B
        k0 = hd * KV_B
        qt = qb[:, q0 + D_NOPE:q0 + QK_B] * cs
        qr = qt + pltpu.roll(qt, D_ROPE, 1)
        qb_ref[hd, :, 0:D_NOPE] = (qb[:, q0:q0 + D_NOPE] * scale_b).astype(BF16)
        qb_ref[hd, :, D_NOPE:QK_B] = (qr * scale_b).astype(BF16)
        kb_ref[hd, :, 0:D_NOPE] = kvb[:, k0:k0 + D_NOPE].astype(BF16)
        kb_ref[hd, :, D_NOPE:QK_B] = kr
        vt_ref[hd] = kvb[:, k0 + D_NOPE:k0 + KV_B].T.astype(BF16)


def _in_proj(x, w_in, gq, gkv, w_uq, w_ukv, cs, *, layer, ts):
    B, S, D = x.shape
    scale_a = HEAD_DIM ** -0.5 * LOG2E
    scale_b = (D_NOPE + D_ROPE) ** -0.5 * LOG2E
    grid = (B, S // ts)
    out_shape = (
        jax.ShapeDtypeStruct((B, S, QA_COLS), BF16),
        jax.ShapeDtypeStruct((B, S, KA_COLS), BF16),
        jax.ShapeDtypeStruct((B, N_KV_A, HEAD_DIM, S), BF16),
        jax.ShapeDtypeStruct((B, N_HEADS_B, S, QK_B), BF16),
        jax.ShapeDtypeStruct((B, N_HEADS_B, S, QK_B), BF16),
        jax.ShapeDtypeStruct((B, N_HEADS_B, D_V, S), BF16),
    )
    return pl.pallas_call(
        functools.partial(_in_proj_kernel, scale_a=scale_a, scale_b=scale_b),
        grid=grid,
        in_specs=[
            pl.BlockSpec((None, ts, D), lambda b, i: (b, i, 0)),
            _layer_spec(w_in, layer, 2),
            _layer_spec(gq, layer, 2),
            _layer_spec(gkv, layer, 2),
            _layer_spec(w_uq, layer, 2),
            _layer_spec(w_ukv, layer, 2),
            pl.BlockSpec((ts, LANES), lambda b, i: (i, 0)),
        ],
        out_specs=(
            pl.BlockSpec((None, ts, QA_COLS), lambda b, i: (b, i, 0)),
            pl.BlockSpec((None, ts, KA_COLS), lambda b, i: (b, i, 0)),
            pl.BlockSpec((None, N_KV_A, HEAD_DIM, ts), lambda b, i: (b, 0, 0, i)),
            pl.BlockSpec((None, N_HEADS_B, ts, QK_B), lambda b, i: (b, 0, i, 0)),
            pl.BlockSpec((None, N_HEADS_B, ts, QK_B), lambda b, i: (b, 0, i, 0)),
            pl.BlockSpec((None, N_HEADS_B, D_V, ts), lambda b, i: (b, 0, 0, i)),
        ),
        out_shape=out_shape,
        compiler_params=_cparams(("parallel", "parallel")),
        name="in_proj",
    )(x, w_in, gq, gkv, w_uq, w_ukv, cs)


def _swa_kernel(sink_ref, q_ref, k_ref, vt_ref, bias_ref, o_ref, *, layer, tq):
    S = k_ref.shape[0]
    nb = S // BLOCK
    qi = pl.program_id(1)
    blocks = [(j, g) for j in range(tq // BLOCK) for g in range(N_KV_A)]

    def window(j):
        n = qi * (tq // BLOCK) + j
        case = jnp.where(n == 0, 0, jnp.where(n == nb - 1, 2, 1))
        rows = [pl.multiple_of(jnp.maximum(n - 1, 0) * BLOCK, BLOCK), pl.multiple_of(n * BLOCK, BLOCK),
                pl.multiple_of(jnp.minimum(n + 1, nb - 1) * BLOCK, BLOCK)]
        return case, rows

    def scores(j, g):
        case, rows = window(j)
        cols = slice(g * HEAD_DIM, (g + 1) * HEAD_DIM)
        kw = jnp.concatenate([k_ref[pl.ds(r, BLOCK), cols] for r in rows], axis=0)
        qs = jnp.concatenate([q_ref[j * BLOCK:(j + 1) * BLOCK, hd * HEAD_DIM:(hd + 1) * HEAD_DIM]
                              for hd in range(g * GROUP_A, (g + 1) * GROUP_A)], axis=0)
        s = lax.dot_general(kw, qs, (((1,), (1,)), ((), ())), preferred_element_type=F32)
        return s + bias_ref[case, g]

    s_next = scores(*blocks[0])
    for idx, (j, g) in enumerate(blocks):
        s = s_next
        if idx + 1 < len(blocks):
            s_next = scores(*blocks[idx + 1])
        _, rows = window(j)
        heads = range(g * GROUP_A, (g + 1) * GROUP_A)
        vwt = jnp.concatenate([vt_ref[g, :, pl.ds(r, BLOCK)] for r in rows], axis=1)
        sink = jnp.concatenate([jnp.full((1, BLOCK), sink_ref[layer, hd] * LOG2E, F32) for hd in heads], axis=1)
        m = jnp.maximum(jnp.max(s, axis=0, keepdims=True), sink)
        p = jnp.exp2(s - m)
        l = jnp.sum(p, axis=0, keepdims=True) + jnp.exp2(sink - m)
        o = jnp.dot(vwt, p.astype(BF16), preferred_element_type=F32) * (1.0 / l)
        for t, hd in enumerate(heads):
            o_ref[j * BLOCK:(j + 1) * BLOCK, hd * HEAD_DIM:(hd + 1) * HEAD_DIM] = (
                o[:, t * BLOCK:(t + 1) * BLOCK].T.astype(o_ref.dtype))


def _swa(sink, qa, ka, vat, bias, *, layer, tq):
    B, S, _ = qa.shape
    grid = (B, S // tq)
    return pl.pallas_call(
        functools.partial(_swa_kernel, layer=layer, tq=tq),
        grid=grid,
        in_specs=[
            pl.BlockSpec(memory_space=pltpu.SMEM),
            pl.BlockSpec((None, tq, QA_COLS), lambda b, i: (b, i, 0)),
            pl.BlockSpec((None, S, KA_COLS), lambda b, i: (b, 0, 0)),
            pl.BlockSpec((None, N_KV_A, HEAD_DIM, S), lambda b, i: (b, 0, 0, 0)),
            pl.BlockSpec(bias.shape, lambda b, i: (0, 0, 0, 0)),
        ],
        out_specs=pl.BlockSpec((None, tq, QA_COLS), lambda b, i: (b, i, 0)),
        out_shape=jax.ShapeDtypeStruct((B, S, QA_COLS), BF16),
        compiler_params=_cparams(("parallel", "parallel")),
        name="swa",
    )(sink, qa, ka, vat, bias)


def _mla_kernel(q_ref, k_ref, vt_ref, o_ref, s0_scr, s1_scr, m0_scr, m1_scr):
    t = pl.program_id(0)
    n_tiles = pl.num_programs(0) - 1

    def score_phase(s_scr, m_scr):
        s = lax.dot_general(k_ref[...], q_ref[...], (((1,), (1,)), ((), ())),
                            preferred_element_type=F32)
        s_scr[...] = s
        m_scr[...] = jnp.max(s, axis=0, keepdims=True)

    def value_phase(s_scr, m_scr):
        p = jnp.exp2(s_scr[...] - m_scr[...])
        l = jnp.sum(p, axis=0, keepdims=True)
        o = jnp.dot(vt_ref[...], p.astype(BF16), preferred_element_type=F32) * (1.0 / l)
        o_ref[...] = o.T.astype(o_ref.dtype)

    bufs = ((s0_scr, m0_scr), (s1_scr, m1_scr))
    for parity in range(2):
        mine, other = bufs[parity], bufs[1 - parity]

        @pl.when(jnp.logical_and(t % 2 == parity, t == 0))
        def _():
            score_phase(*mine)

        @pl.when(jnp.logical_and(t % 2 == parity, jnp.logical_and(t > 0, t < n_tiles)))
        def _():
            score_phase(*mine)
            value_phase(*other)

        @pl.when(jnp.logical_and(t % 2 == parity, t == n_tiles))
        def _():
            value_phase(*other)


def _mla(qb, kb, vt, *, tq):
    B, H, S, _ = qb.shape
    nq = S // tq
    n_tiles = B * H * nq

    def tile(x):
        return x // (H * nq), (x // nq) % H, x % nq

    def cur(t):
        return tile(jnp.minimum(t, n_tiles - 1))

    def prev(t):
        return tile(jnp.maximum(t - 1, 0))

    return pl.pallas_call(
        _mla_kernel,
        grid=(n_tiles + 1,),
        in_specs=[
            pl.BlockSpec((None, None, tq, QK_B), lambda t: (cur(t)[0], cur(t)[1], cur(t)[2], 0)),
            pl.BlockSpec((None, None, S, QK_B), lambda t: (cur(t)[0], cur(t)[1], 0, 0)),
            pl.BlockSpec((None, None, D_V, S), lambda t: (prev(t)[0], prev(t)[1], 0, 0)),
        ],
        out_specs=pl.BlockSpec((None, tq, D_V), lambda t: (prev(t)[0], prev(t)[2], prev(t)[1])),
        out_shape=jax.ShapeDtypeStruct((B, S, H * D_V), BF16),
        scratch_shapes=[pltpu.VMEM((S, tq), F32), pltpu.VMEM((S, tq), F32),
                        pltpu.VMEM((1, tq), F32), pltpu.VMEM((1, tq), F32)],
        compiler_params=_cparams(("arbitrary",)),
        name="mla",
    )(qb, kb, vt)


def _layer_norm(y, g, b):
    mu = jnp.mean(y, axis=-1, keepdims=True)
    d = y - mu
    var = jnp.mean(d * d, axis=-1, keepdims=True)
    return d * lax.rsqrt(var + LN_EPS) * g + b


def _out_ln_kernel(oa_ref, ob_ref, x_ref, wo_ref, g_ref, b_ref, o_ref, *, alpha, n_sub):
    ts = x_ref.shape[0]
    rs = ts // n_sub
    for r in range(n_sub):
        rows = slice(r * rs, (r + 1) * rs)
        o_cat = jnp.concatenate([oa_ref[rows, :], ob_ref[rows, :]], axis=-1)
        attn = jnp.dot(o_cat, wo_ref[...], preferred_element_type=F32)
        y = alpha * x_ref[rows, :] + attn
        o_ref[rows, :] = _layer_norm(y, g_ref[...], b_ref[...])


def _out_ln(oa, ob, x, w_o, g, b, *, layer, alpha, ts, n_sub):
    B, S, D = x.shape
    na, nb = oa.shape[-1], ob.shape[-1]
    grid = (B, S // ts)
    return pl.pallas_call(
        functools.partial(_out_ln_kernel, alpha=alpha, n_sub=n_sub),
        grid=grid,
        in_specs=[
            pl.BlockSpec((None, ts, na), lambda b_, i: (b_, i, 0)),
            pl.BlockSpec((None, ts, nb), lambda b_, i: (b_, i, 0)),
            pl.BlockSpec((None, ts, D), lambda b_, i: (b_, i, 0)),
            _layer_spec(w_o, layer, 2),
            _layer_spec(g, layer, 2),
            _layer_spec(b, layer, 2),
        ],
        out_specs=pl.BlockSpec((None, ts, D), lambda b_, i: (b_, i, 0)),
        out_shape=jax.ShapeDtypeStruct((B, S, D), F32),
        compiler_params=_cparams(("parallel", "parallel")),
        name="out_ln",
    )(oa, ob, x, w_o, g, b)


HALO = 8


def _dot_row_halves(a, w):
    q = a.shape[0] // 3 // BF16_SUBLANES * BF16_SUBLANES
    cuts = [0, q, 2 * q, a.shape[0]]
    return jnp.concatenate([jnp.dot(a[lo:hi], w, preferred_element_type=F32)
                            for lo, hi in zip(cuts[:-1], cuts[1:])], axis=0)


def _ffn_kernel(xp_ref, x_ref, xn_ref, wg_ref, wv_ref, cwg_ref, cwv_ref, cbg_ref, cbv_ref, wd_ref,
                g_ref, b_ref, o_ref, xb_scr, act_scr, *, alpha):
    i = pl.program_id(1)
    c = pl.program_id(2)
    nt = pl.num_programs(1)
    nck = pl.num_programs(2) - 1
    tm = x_ref.shape[0]

    def up_phase():
        xb = xb_scr[...]
        row = lax.broadcasted_iota(jnp.int32, (tm, 1), 0)
        has_prev = (i > 0).astype(F32)
        has_next = (i < nt - 1).astype(F32)

        def conv_branch(w_ref, cw_ref, cb_ref):
            u_ext = _dot_row_halves(xb, w_ref[...])
            u = u_ext[0:tm]
            u_prev = u_ext[tm + HALO - 1:tm + HALO] * has_prev
            u_next = u_ext[tm + HALO:tm + HALO + 1] * has_next
            u_dn = jnp.where(row == 0, u_prev, pltpu.roll(u, 1, 0))
            u_up = jnp.where(row == tm - 1, u_next, pltpu.roll(u, tm - 1, 0))
            return u_dn * cw_ref[0:1, :] + u * cw_ref[1:2, :] + u_up * cw_ref[2:3, :] + cb_ref[...]

        gate = conv_branch(wg_ref, cwg_ref, cbg_ref)
        val = conv_branch(wv_ref, cwv_ref, cbv_ref)
        act_scr[c % 2] = (gate * (1.0 / (1.0 + jnp.exp(-gate))) * val).astype(BF16)

    def down_part():
        return _dot_row_halves(act_scr[(c + 1) % 2], wd_ref[...])

    @pl.when(c == 0)
    def _():
        xb_scr[0:tm, :] = x_ref[...].astype(BF16)
        xb_scr[tm:tm + 2 * HALO, :] = jnp.concatenate([xp_ref[...], xn_ref[...]], axis=0).astype(BF16)
        o_ref[...] = jnp.zeros_like(o_ref)
        up_phase()

    @pl.when(jnp.logical_and(c > 0, c < nck))
    def _():
        up_phase()
        o_ref[...] += down_part()

    @pl.when(c == nck)
    def _():
        y = alpha * x_ref[...] + (o_ref[...] + down_part())
        o_ref[...] = _layer_norm(y, g_ref[...], b_ref[...])


def _ffn(x, w_up, conv_w, conv_b, w_down, g, b, *, layer, alpha, tm, ck):
    B, S, D = x.shape
    F = w_down.shape[1]
    nck = F // ck
    nh = tm // HALO
    grid = (B, S // tm, nck + 1)
    last_halo = S // HALO - 1
    up_idx = lambda c: jnp.minimum(c, nck - 1)
    dn_idx = lambda c: jnp.maximum(c - 1, 0)
    return pl.pallas_call(
        functools.partial(_ffn_kernel, alpha=alpha),
        grid=grid,
        in_specs=[
            pl.BlockSpec((None, HALO, D), lambda b_, i, c: (b_, jnp.maximum(i * nh - 1, 0), 0)),
            pl.BlockSpec((None, tm, D), lambda b_, i, c: (b_, i, 0)),
            pl.BlockSpec((None, HALO, D), lambda b_, i, c: (b_, jnp.minimum((i + 1) * nh, last_halo), 0)),
            pl.BlockSpec((None, D, ck), lambda b_, i, c: (layer, 0, up_idx(c))),
            pl.BlockSpec((None, D, ck), lambda b_, i, c: (layer, 0, nck + up_idx(c))),
            pl.BlockSpec((None, 3, ck), lambda b_, i, c: (layer, 0, up_idx(c))),
            pl.BlockSpec((None, 3, ck), lambda b_, i, c: (layer, 0, nck + up_idx(c))),
            pl.BlockSpec((None, 1, ck), lambda b_, i, c: (layer, 0, up_idx(c))),
            pl.BlockSpec((None, 1, ck), lambda b_, i, c: (layer, 0, nck + up_idx(c))),
            pl.BlockSpec((None, ck, D), lambda b_, i, c: (layer, dn_idx(c), 0)),
            _layer_spec(g, layer, 3),
            _layer_spec(b, layer, 3),
        ],
        out_specs=pl.BlockSpec((None, tm, D), lambda b_, i, c: (b_, i, 0)),
        out_shape=jax.ShapeDtypeStruct((B, S, D), F32),
        scratch_shapes=[pltpu.VMEM((tm + 2 * HALO, D), BF16), pltpu.VMEM((2, tm, ck), BF16)],
        compiler_params=_cparams(("parallel", "parallel", "arbitrary")),
        name="ffn",
    )(x, x, x, w_up, w_up, conv_w, conv_w, conv_b, conv_b, w_down, g, b)


def _t5_bucket(rel):
    half = N_BUCKETS // 2
    max_exact = half // 2
    ret = (rel > 0).astype(jnp.int32) * half
    n = jnp.abs(rel)
    large = max_exact + (jnp.log(jnp.maximum(n, 1).astype(jnp.float32) / max_exact)
                         / math.log(MAX_DISTANCE / max_exact) * (half - max_exact)).astype(jnp.int32)
    large = jnp.minimum(large, half - 1)
    return ret + jnp.where(n < max_exact, n, large)


def _window_bias(rel_bias):
    qi = jnp.arange(BLOCK, dtype=jnp.int32)
    ki = jnp.arange(3 * BLOCK, dtype=jnp.int32)
    rel = ki[None, :] - BLOCK - qi[:, None]
    bucket = _t5_bucket(rel)
    table = rel_bias.astype(F32).T
    hit = bucket[None, :, :, None] == jnp.arange(N_BUCKETS, dtype=jnp.int32)
    bias = jnp.sum(jnp.where(hit, table[:, None, None, :], 0.0), axis=-1)
    base = jnp.where((jnp.abs(rel) <= WINDOW)[None], bias, NEG_INF)
    first = jnp.where((ki >= BLOCK)[None, None, :], base, NEG_INF)
    last = jnp.where((ki < 2 * BLOCK)[None, None, :], base, NEG_INF)
    tables = jnp.stack([first, base, last]) * LOG2E
    tables = tables.reshape(3, N_KV_A, GROUP_A, BLOCK, 3 * BLOCK).transpose(0, 1, 4, 2, 3)
    return tables.reshape(3, N_KV_A, 3 * BLOCK, GROUP_A * BLOCK)


def _rope_table(S):
    inv = 1.0 / (ROPE_THETA ** (jnp.arange(0, D_ROPE, 2, dtype=F32) / D_ROPE))
    ang = jnp.arange(S, dtype=F32)[:, None] * inv[None, :]
    cos, sin = jnp.cos(ang), jnp.sin(ang)
    return jnp.concatenate([cos, cos, -sin, sin], axis=-1)


def _swap_halves(w):
    return jnp.concatenate([w[..., D_ROPE // 2:], w[..., :D_ROPE // 2]], axis=-1)


def _tiles(S):
    mla_elems = 4 * 1024 * 1024
    tq_mla = max(2 * LANES, min(S, mla_elems // S))
    return dict(ts=512, tq_swa=min(2048, S), tq_mla=tq_mla, ts_out=512, sub_out=2,
                tm_ffn=min(1024, S), ck_ffn=512)


def kernel(x_prompt, x_sample, rel_bias, w_in, sink, q_norm_g, w_uq, kv_norm_g, w_ukv, w_o,
           ln1_g, ln1_b, w_up, conv_w, conv_b, w_down, ln2_g, ln2_b):
    depth = w_in.shape[0]
    alpha = (2 * depth) ** 0.25
    in_cols = w_in.shape[-1]
    w_in_b = w_in.astype(BF16)
    w_in_x = jnp.concatenate([w_in_b, _swap_halves(w_in_b[..., in_cols - D_ROPE:])], axis=-1)
    wq = w_uq.astype(BF16).reshape(depth, Q_LORA, N_HEADS_B, D_NOPE + D_ROPE)
    w_uq_x = jnp.concatenate([wq, _swap_halves(wq[..., D_NOPE:])], axis=-1)
    w_uq_x = w_uq_x.reshape(depth, Q_LORA, N_HEADS_B * QK_B)
    w_ukv_b = w_ukv.astype(BF16)
    w_o_b = w_o.astype(BF16)
    w_up_b = w_up.astype(BF16)
    w_down_b = w_down.astype(BF16)
    row = lambda p: p[:, None, :]
    gq, gkv = row(q_norm_g), row(kv_norm_g)
    g1, b1, g2, b2 = row(ln1_g), row(ln1_b), row(ln2_g), row(ln2_b)
    cb = row(conv_b)
    bias = _window_bias(rel_bias)

    def trunk(x):
        S = x.shape[1]
        t = _tiles(S)
        assert S >= 2 * BLOCK and all(S % t[k] == 0 for k in ("ts", "tq_swa", "tq_mla", "ts_out", "tm_ffn")), (S, t)
        cs = _rope_table(S)
        for l in range(depth):
            qa, ka, vat, qb, kb, vt = _in_proj(x, w_in_x, gq, gkv, w_uq_x, w_ukv_b, cs, layer=l, ts=t["ts"])
            oa = _swa(sink, qa, ka, vat, bias, layer=l, tq=t["tq_swa"])
            ob = _mla(qb, kb, vt, tq=t["tq_mla"])
            x = _out_ln(oa, ob, x, w_o_b, g1, b1, layer=l, alpha=alpha, ts=t["ts_out"], n_sub=t["sub_out"])
            x = _ffn(x, w_up_b, conv_w, cb, w_down_b, g2, b2, layer=l, alpha=alpha,
                     tm=t["tm_ffn"], ck=t["ck_ffn"])
        return x

    return trunk(x_prompt), trunk(x_sample)
```
